```python
import math
import jax, jax.numpy as jnp
from jax import lax
import numpy as np

D_MODEL = 2048
BATCH = 1
SEQ = 16384
DEPTH = 2

HEAD_DIM = 128
NSA_HEADS = D_MODEL // (2 * HEAD_DIM)
NSA_KV_GROUPS = 2
NSA_REP = NSA_HEADS // NSA_KV_GROUPS
NSA_WIDTH = NSA_HEADS * HEAD_DIM
CMP_BLOCK = 32
CMP_STRIDE = 16
CMP_HIDDEN = 256
SEL_BLOCK = 64
SEL_COUNT = 16
WINDOW = 512
Q_BLOCK = 128

GDN_QK_HEADS = 4
GDN_V_HEADS = D_MODEL // (2 * HEAD_DIM)
GDN_WIDTH = GDN_V_HEADS * HEAD_DIM
GDN_CONV = 4
GDN_CHUNK = 64
GDN_QK = GDN_QK_HEADS * HEAD_DIM
GDN_CONV_CH = 2 * GDN_QK + GDN_WIDTH

MIX_WIDTH = NSA_WIDTH + GDN_WIDTH
NSA_KV = NSA_KV_GROUPS * HEAD_DIM
IN_SIZES = (NSA_WIDTH, NSA_KV, NSA_KV, NSA_KV, NSA_KV, NSA_KV, NSA_KV, 3 * NSA_HEADS,
            GDN_QK, GDN_QK, GDN_WIDTH, GDN_WIDTH, GDN_V_HEADS, GDN_V_HEADS)
IN_WIDTH = sum(IN_SIZES)

N_EXPERTS = 32
TOP_K = 4
D_FF = D_MODEL
SWIGLU_LIMIT = 7.0
SWIGLU_ALPHA = 1.702
EXPERT_BLOCK = 256

DEEPNORM_ALPHA = (2 * DEPTH) ** 0.25
DEEPNORM_BETA = (8 * DEPTH) ** -0.25

kernel_name = "hymba_nsa_gdn_moe_deepnorm_adaln"


def _ln_plain(x, eps=1e-6):
    xf = x.astype(jnp.float32)
    mu = jnp.mean(xf, -1, keepdims=True)
    var = jnp.mean(jnp.square(xf - mu), -1, keepdims=True)
    return ((xf - mu) * lax.rsqrt(var + eps)).astype(x.dtype)


def _ln_affine(x, g, b, eps=1e-5):
    xf = x.astype(jnp.float32)
    mu = jnp.mean(xf, -1, keepdims=True)
    var = jnp.mean(jnp.square(xf - mu), -1, keepdims=True)
    return ((xf - mu) * lax.rsqrt(var + eps) * g + b).astype(x.dtype)


def _rmsnorm(x, g, eps=1e-6):
    xf = x.astype(jnp.float32)
    return (xf * lax.rsqrt(jnp.mean(xf * xf, -1, keepdims=True) + eps) * g).astype(x.dtype)


def _l2norm(x, eps=1e-6):
    xf = x.astype(jnp.float32)
    return xf * lax.rsqrt(jnp.sum(xf * xf, -1, keepdims=True) + eps)


def _masked_softmax(s, mask, axis):
    s = jnp.where(mask, s.astype(jnp.float32), -jnp.inf)
    m = jnp.max(s, axis=axis, keepdims=True)
    m = jnp.where(jnp.isfinite(m), m, 0.0)
    p = jnp.exp(s - m)
    return p / jnp.maximum(jnp.sum(p, axis=axis, keepdims=True), 1e-30)


def _modulate(x, shift, scale):
    return _ln_plain(x) * (1.0 + scale[:, None, :]) + shift[:, None, :]


def _compress(kv, pos_emb, w1, w2):
    B, T, G, dh = kv.shape
    n_c = (T - CMP_BLOCK) // CMP_STRIDE + 1
    idx = jnp.arange(n_c)[:, None] * CMP_STRIDE + jnp.arange(CMP_BLOCK)[None, :]
    blk = kv[:, idx] + pos_emb[None, None, :, None, :]
    blk = blk.transpose(0, 1, 3, 2, 4).reshape(B, n_c, G, CMP_BLOCK * dh)
    return jax.nn.gelu(blk @ w1) @ w2


def _nsa(q, kc, vc, ks, vs, kw, vw, gates):
    B, T, H, dh = q.shape
    G, R = NSA_KV_GROUPS, NSA_REP
    scale = dh ** -0.5
    n_c = kc.shape[1]
    n_s = T // SEL_BLOCK
    n_sel = min(SEL_COUNT, n_s)
    cmp_end = jnp.arange(n_c) * CMP_STRIDE + CMP_BLOCK - 1
    ci = jnp.arange(n_c)[:, None] * CMP_STRIDE
    sj = jnp.arange(n_s)[None, :] * SEL_BLOCK
    overlap = ((ci < sj + SEL_BLOCK) & (ci + CMP_BLOCK > sj)).astype(jnp.float32)
    ks_blocks = ks.reshape(B, n_s, SEL_BLOCK, G, dh).transpose(0, 3, 1, 2, 4)
    vs_blocks = vs.reshape(B, n_s, SEL_BLOCK, G, dh).transpose(0, 3, 1, 2, 4)
    kw_pad = jnp.pad(kw, ((0, 0), (WINDOW, 0), (0, 0), (0, 0)))
    vw_pad = jnp.pad(vw, ((0, 0), (WINDOW, 0), (0, 0), (0, 0)))
    bi = jnp.arange(B)[:, None, None, None]
    gi = jnp.arange(G)[None, :, None, None]
    blk_ids = jnp.arange(n_s)

    def one_block(qb):
        t0 = qb * Q_BLOCK
        t = t0 + jnp.arange(Q_BLOCK)
        qq = lax.dynamic_slice_in_dim(q, t0, Q_BLOCK, axis=1).reshape(B, Q_BLOCK, G, R, dh)
        gg = lax.dynamic_slice_in_dim(gates, t0, Q_BLOCK, axis=1).reshape(B, Q_BLOCK, G, R, 3)
        s = jnp.einsum('bqgrd,bcgd->bgrqc', qq, kc) * scale
        p_cmp = _masked_softmax(s, cmp_end[None, :] <= t[:, None], -1)
        o_cmp = jnp.einsum('bgrqc,bcgd->bqgrd', p_cmp.astype(vc.dtype), vc)
        imp = jnp.einsum('bgrqc,cs->bgqs', p_cmp, overlap)
        cur = t // SEL_BLOCK
        valid = blk_ids[None, :] <= cur[:, None]
        forced = (blk_ids[None, :] == 0) | (blk_ids[None, :] == cur[:, None]) | (blk_ids[None, :] == cur[:, None] - 1)
        score = jnp.where(valid, imp + jnp.where(forced, jnp.inf, 0.0), -jnp.inf)
        top_s, top_i = lax.top_k(score, n_sel)
        sel_ok = top_s > -jnp.inf
        kg = ks_blocks[bi, gi, top_i]
        vg = vs_blocks[bi, gi, top_i]
        kpos = top_i[..., None] * SEL_BLOCK + jnp.arange(SEL_BLOCK)
        m_sel = sel_ok[..., None] & (kpos <= t[None, None, :, None, None])
        s = jnp.einsum('bqgrd,bgqnkd->bgrqnk', qq, kg) * scale
        p = _masked_softmax(s, m_sel[:, :, None], (-2, -1))
        o_sel = jnp.einsum('bgrqnk,bgqnkd->bqgrd', p.astype(vg.dtype), vg)
        kk = lax.dynamic_slice_in_dim(kw_pad, t0, Q_BLOCK + WINDOW, axis=1)
        vv = lax.dynamic_slice_in_dim(vw_pad, t0, Q_BLOCK + WINDOW, axis=1)
        kp = t0 - WINDOW + jnp.arange(Q_BLOCK + WINDOW)
        m_win = (kp[None, :] >= 0) & (kp[None, :] <= t[:, None]) & (kp[None, :] > t[:, None] - WINDOW)
        s = jnp.einsum('bqgrd,bkgd->bgrqk', qq, kk) * scale
        p = _masked_softmax(s, m_win, -1)
        o_win = jnp.einsum('bgrqk,bkgd->bqgrd', p.astype(vv.dtype), vv)
        o = gg[..., 0:1] * o_cmp + gg[..., 1:2] * o_sel + gg[..., 2:3] * o_win
        return o.reshape(B, Q_BLOCK, H * dh).astype(q.dtype)

    out = lax.map(one_block, jnp.arange(T // Q_BLOCK))
    return out.transpose(1, 0, 2, 3).reshape(B, T, H * dh)


def _gated_delta_chunked(q, k, v, g, beta):
    B, T, H, dk = q.shape
    dv = v.shape[-1]
    C = GDN_CHUNK
    N = T // C
    f32 = jnp.float32

    def to_chunks(a):
        return a.astype(f32).reshape(B, N, C, H, -1).transpose(1, 0, 3, 2, 4)

    q = to_chunks(q) * (dk ** -0.5)
    k = to_chunks(k)
    v = to_chunks(v)
    g = g.astype(f32).reshape(B, N, C, H).transpose(1, 0, 3, 2)
    beta = beta.astype(f32).reshape(B, N, C, H).transpose(1, 0, 3, 2)
    gc = jnp.cumsum(g, axis=-1)
    tri = jnp.tril(jnp.ones((C, C), bool))
    strict = jnp.tril(jnp.ones((C, C), bool), -1)
    diff = gc[..., :, None] - gc[..., None, :]
    decay = jnp.where(tri, jnp.exp(jnp.where(tri, diff, 0.0)), 0.0)
    kb = k * beta[..., None]
    a_mat = jnp.eye(C, dtype=f32) + jnp.where(strict, jnp.einsum('nbhcd,nbhsd->nbhcs', kb, k) * decay, 0.0)
    u = lax.linalg.triangular_solve(a_mat, v * beta[..., None], left_side=True, lower=True, unit_diagonal=True)
    w = lax.linalg.triangular_solve(a_mat, kb * jnp.exp(gc)[..., None], left_side=True, lower=True, unit_diagonal=True)
    qk = jnp.where(tri, jnp.einsum('nbhcd,nbhsd->nbhcs', q, k) * decay, 0.0)
    q_dec = q * jnp.exp(gc)[..., None]
    k_dec = k * jnp.exp(gc[..., -1:] - gc)[..., None]
    g_tot = jnp.exp(gc[..., -1])

    def step(S, xs):
        u_n, w_n, qk_n, q_n, k_n, gt = xs
        v_new = u_n - jnp.einsum('bhcd,bhde->bhce', w_n, S)
        o = jnp.einsum('bhcd,bhde->bhce', q_n, S) + jnp.einsum('bhcs,bhse->bhce', qk_n, v_new)
        S = S * gt[..., None, None] + jnp.einsum('bhcd,bhce->bhde', k_n, v_new)
        return S, o

    S0 = jnp.zeros((B, H, dk, dv), f32)
    _, o = lax.scan(step, S0, (u, w, qk, q_dec, k_dec, g_tot))
    return o.transpose(1, 0, 3, 2, 4).reshape(B, T, H, dv)


def _gdn(qg, kg, vg, z, b_logit, a_logit, conv_w, a_log, dt_bias, norm_g):
    B, T, _ = qg.shape
    qkv = jnp.concatenate([qg, kg, vg], axis=-1)
    qkv = jax.nn.silu(lax.conv_general_dilated(
        qkv, conv_w, window_strides=(1,), padding=[(GDN_CONV - 1, 0)],
        dimension_numbers=('NWC', 'WIO', 'NWC'), feature_group_count=GDN_CONV_CH))
    qg, kg, vg = jnp.split(qkv, [GDN_QK, 2 * GDN_QK], axis=-1)
    rep = GDN_V_HEADS // GDN_QK_HEADS
    q = jnp.repeat(_l2norm(qg.reshape(B, T, GDN_QK_HEADS, HEAD_DIM)), rep, axis=2)
    k = jnp.repeat(_l2norm(kg.reshape(B, T, GDN_QK_HEADS, HEAD_DIM)), rep, axis=2)
    v = vg.reshape(B, T, GDN_V_HEADS, HEAD_DIM)
    beta = jax.nn.sigmoid(b_logit.astype(jnp.float32))
    g = -jnp.exp(a_log.astype(jnp.float32)) * jax.nn.softplus(a_logit.astype(jnp.float32) + dt_bias.astype(jnp.float32))
    o = _gated_delta_chunked(q, k, v, g, beta)
    o = _rmsnorm(o, norm_g.astype(jnp.float32)) * jax.nn.silu(z.reshape(B, T, GDN_V_HEADS, HEAD_DIM).astype(jnp.float32))
    return o.reshape(B, T, GDN_WIDTH).astype(qg.dtype)


def _hybrid_mixer(h, w_in, cmp_pos_k, cmp_pos_v, cmp_k_w1, cmp_k_w2, cmp_v_w1, cmp_v_w2,
                  nsa_out_g, conv_w, a_log, dt_bias, gdn_norm_g, w_out):
    B, T, _ = h.shape
    proj = h @ w_in
    offs = np.cumsum(IN_SIZES)[:-1].tolist()
    (q, kc, vc, ks, vs, kw, vw, gate_l, qg, kg, vg, z, b_l, a_l) = jnp.split(proj, offs, axis=-1)
    kvs = lambda a: a.reshape(B, T, NSA_KV_GROUPS, HEAD_DIM)
    kc_c = _compress(kvs(kc), cmp_pos_k, cmp_k_w1, cmp_k_w2)
    vc_c = _compress(kvs(vc), cmp_pos_v, cmp_v_w1, cmp_v_w2)
    gates = jax.nn.sigmoid(gate_l).reshape(B, T, NSA_HEADS, 3)
    o_nsa = _nsa(q.reshape(B, T, NSA_HEADS, HEAD_DIM), kc_c, vc_c, kvs(ks), kvs(vs), kvs(kw), kvs(vw), gates)
    o_nsa = _rmsnorm(o_nsa.reshape(B, T, NSA_HEADS, HEAD_DIM), nsa_out_g.reshape(NSA_HEADS, HEAD_DIM)).reshape(B, T, NSA_WIDTH)
    o_gdn = _gdn(qg, kg, vg, z, b_l, a_l, conv_w, a_log, dt_bias, gdn_norm_g)
    return jnp.concatenate([o_nsa, o_gdn], axis=-1) @ w_out


def _moe(h, router_w, router_b, w_gu, b_gu, w_dn, b_dn):
    B, T, D = h.shape
    NT = B * T
    xt = h.reshape(NT, D)
    logits = (xt @ router_w + router_b).astype(jnp.float32)
    top_v, top_i = lax.top_k(logits, TOP_K)
    gate = jax.nn.softmax(top_v, axis=-1)
    flat_e = top_i.reshape(-1)
    flat_tok = jnp.arange(NT * TOP_K, dtype=jnp.int32) // TOP_K
    flat_g = gate.reshape(-1)
    order = jnp.argsort(flat_e)
    se, stok, sg = flat_e[order], flat_tok[order], flat_g[order]
    counts = jnp.bincount(flat_e, length=N_EXPERTS)
    padded = (counts + EXPERT_BLOCK - 1) // EXPERT_BLOCK * EXPERT_BLOCK
    pad_end = jnp.cumsum(padded)
    pad_start = pad_end - padded
    start = jnp.cumsum(counts) - counts
    dest = pad_start[se] + (jnp.arange(NT * TOP_K) - start[se])
    NB = -(-(NT * TOP_K) // EXPERT_BLOCK) + N_EXPERTS
    row_tok = jnp.full((NB * EXPERT_BLOCK,), NT, jnp.int32).at[dest].set(stok)
    row_gate = jnp.zeros((NB * EXPERT_BLOCK,), jnp.float32).at[dest].set(sg)
    blk_exp = jnp.minimum(jnp.searchsorted(pad_end, jnp.arange(NB) * EXPERT_BLOCK, side='right'), N_EXPERTS - 1)
    x_pad = jnp.concatenate([xt, jnp.zeros((1, D), xt.dtype)], axis=0)

    def run_block(args):
        tok, e = args
        xb = x_pad[tok]
        hgu = xb @ w_gu[e] + b_gu[e]
        x_glu = jnp.minimum(hgu[:, ::2], SWIGLU_LIMIT)
        x_lin = jnp.clip(hgu[:, 1::2], -SWIGLU_LIMIT, SWIGLU_LIMIT)
        act = x_glu * jax.nn.sigmoid(SWIGLU_ALPHA * x_glu) * (x_lin + 1.0)
        return act @ w_dn[e] + b_dn[e]

    y = lax.map(run_block, (row_tok.reshape(NB, EXPERT_BLOCK), blk_exp))
    y = y.reshape(NB * EXPERT_BLOCK, D) * row_gate[:, None]
    out = jax.ops.segment_sum(y, row_tok, num_segments=NT + 1)[:NT]
    return out.reshape(B, T, D).astype(h.dtype)


def setup_inputs(seed: int = 0) -> dict:
    key = jax.random.key(seed)
    ks = jax.random.split(key, 32)
    f32 = jnp.float32
    L = DEPTH

    def nrm(k, shape, s):
        return jax.random.normal(k, shape, f32) * s

    dt = jnp.exp(jax.random.uniform(ks[12], (L, GDN_V_HEADS), f32, math.log(1e-3), math.log(1e-1)))
    return {
        "x": nrm(ks[0], (BATCH, SEQ, D_MODEL), 1.0),
        "c": nrm(ks[1], (BATCH, D_MODEL), 1.0),
        "w_in": nrm(ks[2], (L, D_MODEL, IN_WIDTH), D_MODEL ** -0.5),
        "cmp_pos_k": nrm(ks[3], (L, CMP_BLOCK, HEAD_DIM), 0.02),
        "cmp_pos_v": nrm(ks[4], (L, CMP_BLOCK, HEAD_DIM), 0.02),
        "cmp_k_w1": nrm(ks[5], (L, CMP_BLOCK * HEAD_DIM, CMP_HIDDEN), (CMP_BLOCK * HEAD_DIM) ** -0.5),
        "cmp_k_w2": nrm(ks[6], (L, CMP_HIDDEN, HEAD_DIM), CMP_HIDDEN ** -0.5),
        "cmp_v_w1": nrm(ks[7], (L, CMP_BLOCK * HEAD_DIM, CMP_HIDDEN), (CMP_BLOCK * HEAD_DIM) ** -0.5),
        "cmp_v_w2": nrm(ks[8], (L, CMP_HIDDEN, HEAD_DIM), CMP_HIDDEN ** -0.5),
        "nsa_out_g": 1.0 + nrm(ks[9], (L, NSA_WIDTH), 0.02),
        "gdn_conv_w": nrm(ks[10], (L, GDN_CONV, 1, GDN_CONV_CH), GDN_CONV ** -0.5),
        "gdn_a_log": jnp.log(jax.random.uniform(ks[11], (L, GDN_V_HEADS), f32, 1.0, 16.0)),
        "gdn_dt_bias": dt + jnp.log(-jnp.expm1(-dt)),
        "gdn_norm_g": 1.0 + nrm(ks[13], (L, HEAD_DIM), 0.02),
        "w_out": nrm(ks[14], (L, MIX_WIDTH, D_MODEL), MIX_WIDTH ** -0.5 * DEEPNORM_BETA),
        "w_ada": nrm(ks[15], (L, D_MODEL, 6 * D_MODEL), 0.1 * D_MODEL ** -0.5),
        "b_ada": nrm(ks[16], (L, 6 * D_MODEL), 0.01),
        "ln_mix_g": 1.0 + nrm(ks[17], (L, D_MODEL), 0.02),
        "ln_mix_b": nrm(ks[18], (L, D_MODEL), 0.02),
        "router_w": nrm(ks[19], (L, D_MODEL, N_EXPERTS), D_MODEL ** -0.5),
        "router_b": nrm(ks[20], (L, N_EXPERTS), 0.01),
        "w_gu": nrm(ks[21], (L, N_EXPERTS, D_MODEL, 2 * D_FF), D_MODEL ** -0.5),
        "b_gu": nrm(ks[22], (L, N_EXPERTS, 2 * D_FF), 0.01),
        "w_dn": nrm(ks[23], (L, N_EXPERTS, D_FF, D_MODEL), D_FF ** -0.5 * DEEPNORM_BETA),
        "b_dn": nrm(ks[24], (L, N_EXPERTS, D_MODEL), 0.01),
        "ln_ffn_g": 1.0 + nrm(ks[25], (L, D_MODEL), 0.02),
        "ln_ffn_b": nrm(ks[26], (L, D_MODEL), 0.02),
    }


def reference(x, c, w_in, cmp_pos_k, cmp_pos_v, cmp_k_w1, cmp_k_w2, cmp_v_w1, cmp_v_w2,
              nsa_out_g, gdn_conv_w, gdn_a_log, gdn_dt_bias, gdn_norm_g, w_out, w_ada, b_ada,
              ln_mix_g, ln_mix_b, router_w, router_b, w_gu, b_gu, w_dn, b_dn, ln_ffn_g, ln_ffn_b):
    c_act = jax.nn.silu(c)
    for l in range(DEPTH):
        mod = c_act @ w_ada[l] + b_ada[l]
        sh_m, sc_m, g_m, sh_f, sc_f, g_f = jnp.split(mod, 6, axis=-1)
        h = _modulate(x, sh_m, sc_m)
        mix = _hybrid_mixer(h, w_in[l], cmp_pos_k[l], cmp_pos_v[l], cmp_k_w1[l], cmp_k_w2[l],
                            cmp_v_w1[l], cmp_v_w2[l], nsa_out_g[l], gdn_conv_w[l], gdn_a_log[l],
                            gdn_dt_bias[l], gdn_norm_g[l], w_out[l])
        x = _ln_affine(DEEPNORM_ALPHA * x + (1.0 + g_m)[:, None, :] * mix, ln_mix_g[l], ln_mix_b[l])
        h = _modulate(x, sh_f, sc_f)
        y = _moe(h, router_w[l], router_b[l], w_gu[l], b_gu[l], w_dn[l], b_dn[l])
        x = _ln_affine(DEEPNORM_ALPHA * x + (1.0 + g_f)[:, None, :] * y, ln_ffn_g[l], ln_ffn_b[l])
    return x
```

```python
import functools

import numpy as np
import jax
import jax.numpy as jnp
from jax import lax
from jax.experimental import pallas as pl
from jax.experimental.pallas import tpu as pltpu

F32 = jnp.float32
BF16 = jnp.bfloat16
I32 = jnp.int32
U32 = jnp.uint32

D_MODEL = 2048
DEPTH = 2
HEAD_DIM = 128
NSA_HEADS = 8
NSA_GROUPS = 2
NSA_REP = NSA_HEADS // NSA_GROUPS
CMP_BLOCK = 32
CMP_STRIDE = 16
CMP_HIDDEN = 256
SEL_BLOCK = 64
SEL_COUNT = 16
WINDOW = 512
GDN_QK_HEADS = 4
GDN_V_HEADS = 8
GDN_CONV = 4
GDN_CHUNK = 64
N_EXPERTS = 32
TOP_K = 4
D_FF = 2048
SWIGLU_LIMIT = 7.0
SWIGLU_ALPHA = 1.702
DEEPNORM_ALPHA = (2 * DEPTH) ** 0.25

LANES = 128
VMEM_LIMIT = 56 * 1024 * 1024
NEG = -1e30
SEL_SUPER = LANES * SEL_BLOCK
EXPERT_ROWS = 512

PB_Q, PB_KS, PB_VS, PB_KW, PB_VW = 0, 8, 10, 12, 14
PF_QG, PF_KG, PF_VG, PF_Z, PF_KC, PF_VC, PF_SMALL = 0, 4, 8, 16, 24, 26, 28
SM_GATE, SM_BETA, SM_A = 0, 24, 32


def _cparams(sem):
    return pltpu.CompilerParams(dimension_semantics=sem, vmem_limit_bytes=VMEM_LIMIT)


def _dot(a, b):
    return jnp.dot(a, b, preferred_element_type=F32)


def _dot_nt(a, b):
    return lax.dot_general(a, b, (((1,), (1,)), ((), ())), preferred_element_type=F32)


def _split2(a):
    hi = a.astype(BF16)
    return hi, (a - hi.astype(F32)).astype(BF16)


def _split3(a):
    hi = a.astype(BF16)
    r = a - hi.astype(F32)
    mid = r.astype(BF16)
    return hi, mid, (r - mid.astype(F32)).astype(BF16)


def _dot3(a, b):
    ah, al = _split2(a)
    bh, bl = _split2(b)
    return _dot(ah, bh) + (_dot(ah, bl) + _dot(al, bh))


def _dot_ones_l(ones_bf16, x):
    h, m, l = _split3(x)
    return _dot(ones_bf16, h) + (_dot(ones_bf16, m) + _dot(ones_bf16, l))


def _dot_ones_r(x, ones_bf16):
    h, m, l = _split3(x)
    return _dot(h, ones_bf16) + (_dot(m, ones_bf16) + _dot(l, ones_bf16))


def _ln_plain(x, eps):
    mu = jnp.mean(x, -1, keepdims=True)
    xc = x - mu
    var = jnp.mean(xc * xc, -1, keepdims=True)
    return xc * lax.rsqrt(var + eps)


def _masked_softmax(s, mask):
    s = jnp.where(mask, s, -jnp.inf)
    m = jnp.max(s, -1, keepdims=True)
    m = jnp.where(jnp.abs(m) < jnp.inf, m, 0.0)
    p = jnp.exp(s - m)
    return p / jnp.maximum(jnp.sum(p, -1, keepdims=True), 1e-30)


def _iota(shape, dim):
    return lax.broadcasted_iota(I32, shape, dim)


def _stack_heads(q):
    return jnp.concatenate([q[:, h * HEAD_DIM:(h + 1) * HEAD_DIM] for h in range(NSA_REP)], axis=0)


def _unstack_heads(o, t):
    return jnp.concatenate([o[h * t:(h + 1) * t] for h in range(NSA_REP)], axis=1)


def _ada_body(c_ref, w_ref, b_ref, o_ref):
    c = c_ref[...]
    ca = c * jax.nn.sigmoid(c)
    o_ref[0] = jnp.sum(w_ref[0] * ca, axis=0, keepdims=True) + b_ref[0]


def _ada(c, w_ada, b_ada):
    nl, d, n = w_ada.shape
    tn = 1024
    return pl.pallas_call(
        _ada_body,
        grid=(nl, n // tn),
        in_specs=[pl.BlockSpec((d, 1), lambda l, j: (0, 0)),
                  pl.BlockSpec((1, d, tn), lambda l, j: (l, 0, j)),
                  pl.BlockSpec((1, 1, tn), lambda l, j: (l, 0, j))],
        out_specs=pl.BlockSpec((1, 1, tn), lambda l, j: (l, 0, j)),
        out_shape=jax.ShapeDtypeStruct((nl, 1, n), F32),
        compiler_params=_cparams(("arbitrary", "arbitrary")),
        name="ada_mod",
    )(c.reshape(d, 1), w_ada, b_ada.reshape(nl, 1, n))


def _lnmod_body(x_ref, sh_ref, sc_ref, o_ref):
    h = _ln_plain(x_ref[...], 1e-6) * (1.0 + sc_ref[...]) + sh_ref[...]
    o_ref[...] = h.astype(o_ref.dtype)


def _lnmod(x, shift, scale):
    t, d = x.shape
    tm = 512
    vec = pl.BlockSpec((1, d), lambda i: (0, 0))
    return pl.pallas_call(
        _lnmod_body,
        grid=(t // tm,),
        in_specs=[pl.BlockSpec((tm, d), lambda i: (i, 0)), vec, vec],
        out_specs=pl.BlockSpec((tm, d), lambda i: (i, 0)),
        out_shape=jax.ShapeDtypeStruct((t, d), BF16),
        compiler_params=_cparams(("arbitrary",)),
        name="ln_mod",
    )(x, shift, scale)


def _mm_body(a_ref, w_ref, cs_ref, o_ref):
    o_ref[...] = (_dot(a_ref[...], w_ref[...]) * cs_ref[...]).astype(o_ref.dtype)


def _matmul(a, w, colscale, out_dtype):
    m, k = a.shape
    n = w.shape[1]
    tm, tn = 1024, 512
    return pl.pallas_call(
        _mm_body,
        grid=(m // tm, n // tn),
        in_specs=[pl.BlockSpec((tm, k), lambda i, j: (i, 0)),
                  pl.BlockSpec((k, tn), lambda i, j: (0, j)),
                  pl.BlockSpec((1, tn), lambda i, j: (0, j))],
        out_specs=pl.BlockSpec((tm, tn), lambda i, j: (i, j)),
        out_shape=jax.ShapeDtypeStruct((m, n), out_dtype),
        compiler_params=_cparams(("arbitrary", "arbitrary")),
        name="proj_mm",
    )(a, w, colscale)


def _compress_body(x_ref, p_ref, w1_ref, w2_ref, o_ref):
    x = x_ref[0]
    nc = x.shape[0]
    y = _dot((x + p_ref[0, 0]).astype(BF16), w1_ref[0, 0])
    z = _dot((x + p_ref[0, 1]).astype(BF16), w1_ref[0, 1])
    z_next = pltpu.roll(z, nc - 1, 0)
    hid = jax.nn.gelu(y + z_next)
    o_ref[0] = _dot(hid.astype(BF16), w2_ref[0])


def _compress(x4, pos, w1, w2):
    n4, nc, wdt = x4.shape
    return pl.pallas_call(
        _compress_body,
        grid=(n4,),
        in_specs=[pl.BlockSpec((1, nc, wdt), lambda i: (i, 0, 0)),
                  pl.BlockSpec((1, 2, 1, wdt), lambda i: (i // NSA_GROUPS, 0, 0, 0)),
                  pl.BlockSpec((1, 2, wdt, CMP_HIDDEN), lambda i: (i // NSA_GROUPS, 0, 0, 0)),
                  pl.BlockSpec((1, CMP_HIDDEN, HEAD_DIM), lambda i: (i // NSA_GROUPS, 0, 0))],
        out_specs=pl.BlockSpec((1, nc, HEAD_DIM), lambda i: (i, 0, 0)),
        out_shape=jax.ShapeDtypeStruct((n4, nc, HEAD_DIM), F32),
        compiler_params=_cparams(("arbitrary",)),
        name="nsa_compress",
    )(x4, pos, w1, w2)


def _cmp_body(q_ref, kc_ref, vc_ref, ov_ref, o_ref, mb_ref, *, tq, n_sel):
    i = pl.program_id(1)
    nc = kc_ref.shape[1]
    nsp = ov_ref.shape[1]
    qs = _stack_heads(q_ref[...])
    s = _dot_nt(qs, kc_ref[0].astype(BF16))
    t_row = i * tq + (_iota((NSA_REP * tq, 1), 0) & (tq - 1))
    cmp_end = _iota((1, nc), 1) * CMP_STRIDE + (CMP_BLOCK - 1)
    p = _masked_softmax(s, cmp_end <= t_row)
    o = _dot(p.astype(BF16), vc_ref[0].astype(BF16))
    o_ref[...] = _unstack_heads(o, tq)
    ps = (p[0:tq] + p[tq:2 * tq]) + (p[2 * tq:3 * tq] + p[3 * tq:4 * tq])
    hi, lo = _split2(ps)
    ov = ov_ref[...]
    imp = _dot(hi, ov) + _dot(lo, ov)
    cur = (i * tq + _iota((tq, 1), 0)) >> 6
    blk = _iota((1, nsp), 1)
    valid = blk <= cur
    forced = (blk == 0) | (blk == cur) | (blk == cur - 1)
    score = jnp.where(valid, imp + jnp.where(forced, jnp.inf, 0.0), -jnp.inf)
    blkf = blk.astype(F32)
    sel = jnp.zeros((tq, nsp), F32)
    for _ in range(n_sel):
        m = jnp.max(score, -1, keepdims=True)
        idx = jnp.min(jnp.where(score == m, blkf, float(nsp)), -1, keepdims=True)
        pick = blkf == idx
        sel = jnp.where(pick & (m > -jnp.inf), 1.0, sel)
        score = jnp.where(pick, -jnp.inf, score)
    mb_ref[0] = jnp.where(sel > 0.0, 0.0, NEG).astype(BF16)


def _cmp_select(pb, cc, ov, n_sel):
    t = pb.shape[0]
    nc, nsp = ov.shape
    tq = 128
    return pl.pallas_call(
        functools.partial(_cmp_body, tq=tq, n_sel=n_sel),
        grid=(NSA_GROUPS, t // tq),
        in_specs=[pl.BlockSpec((tq, NSA_REP * HEAD_DIM), lambda g, i: (i, g)),
                  pl.BlockSpec((1, nc, HEAD_DIM), lambda g, i: (g, 0, 0)),
                  pl.BlockSpec((1, nc, HEAD_DIM), lambda g, i: (NSA_GROUPS + g, 0, 0)),
                  pl.BlockSpec((nc, nsp), lambda g, i: (0, 0))],
        out_specs=[pl.BlockSpec((tq, NSA_REP * HEAD_DIM), lambda g, i: (i, g)),
                   pl.BlockSpec((1, tq, nsp), lambda g, i: (g, i, 0))],
        out_shape=[jax.ShapeDtypeStruct((t, NSA_HEADS * HEAD_DIM), F32),
                   jax.ShapeDtypeStruct((NSA_GROUPS, t, nsp), BF16)],
        compiler_params=_cparams(("arbitrary", "arbitrary")),
        name="nsa_cmp_select",
    )(pb, cc, cc, ov)


def _sel_body(qi_ref, ki_ref, q_ref, mb_ref, k_ref, v_ref, e_ref, o_ref, qa_ref, m_ref, l_ref, acc_ref, *, tq, tk):
    step = pl.program_id(1)
    qi = qi_ref[step]
    ki = ki_ref[step]

    @pl.when(ki == 0)
    def _():
        for h in range(NSA_REP):
            qa_ref[h * tq:(h + 1) * tq, 0:HEAD_DIM] = q_ref[:, h * HEAD_DIM:(h + 1) * HEAD_DIM]
        m_ref[...] = jnp.full(m_ref.shape, NEG, F32)
        l_ref[...] = jnp.zeros(l_ref.shape, F32)
        acc_ref[...] = jnp.zeros(acc_ref.shape, F32)

    @pl.when((ki & (SEL_SUPER // tk - 1)) == 0)
    def _():
        mb = mb_ref[0]
        for h in range(NSA_REP):
            qa_ref[h * tq:(h + 1) * tq, HEAD_DIM:2 * HEAD_DIM] = mb

    kaug = jnp.concatenate([k_ref[...], e_ref[...]], axis=1)
    s = _dot_nt(qa_ref[...], kaug)
    t_row = qi * tq + (_iota((NSA_REP * tq, 1), 0) & (tq - 1))
    col = ki * tk + _iota((1, tk), 1)
    s = jnp.where(col <= t_row, s, NEG)
    m_prev = m_ref[...]
    m_new = jnp.maximum(m_prev, jnp.max(s, -1, keepdims=True))
    p = jnp.exp(s - m_new)
    alpha = jnp.exp(m_prev - m_new)
    l_ref[...] = alpha * l_ref[...] + jnp.sum(p, -1, keepdims=True)
    acc_ref[...] = alpha * acc_ref[...] + _dot(p.astype(BF16), v_ref[...])
    m_ref[...] = m_new

    @pl.when(ki == ((qi + 1) * tq - 1) // tk)
    def _():
        o_ref[...] = _unstack_heads(acc_ref[...] / l_ref[...], tq)


def _sel_attention(pb, mb, etab):
    t = pb.shape[0]
    tq, tk = 256, 512
    qi, ki = [], []
    for a in range(t // tq):
        for b in range(((a + 1) * tq - 1) // tk + 1):
            qi.append(a)
            ki.append(b)
    nsteps = len(qi)
    sup_tiles = SEL_SUPER // tk
    grid_spec = pltpu.PrefetchScalarGridSpec(
        num_scalar_prefetch=2,
        grid=(NSA_GROUPS, nsteps),
        in_specs=[pl.BlockSpec((tq, NSA_REP * HEAD_DIM), lambda g, s, qi, ki: (qi[s], g)),
                  pl.BlockSpec((1, tq, LANES), lambda g, s, qi, ki: (g, qi[s], ki[s] // sup_tiles)),
                  pl.BlockSpec((tk, HEAD_DIM), lambda g, s, qi, ki: (ki[s], PB_KS + g)),
                  pl.BlockSpec((tk, HEAD_DIM), lambda g, s, qi, ki: (ki[s], PB_VS + g)),
                  pl.BlockSpec((tk, LANES), lambda g, s, qi, ki: (ki[s] % sup_tiles, 0))],
        out_specs=pl.BlockSpec((tq, NSA_REP * HEAD_DIM), lambda g, s, qi, ki: (qi[s], g)),
        scratch_shapes=[pltpu.VMEM((NSA_REP * tq, 2 * HEAD_DIM), BF16),
                        pltpu.VMEM((NSA_REP * tq, 1), F32),
                        pltpu.VMEM((NSA_REP * tq, 1), F32),
                        pltpu.VMEM((NSA_REP * tq, HEAD_DIM), F32)],
    )
    return pl.pallas_call(
        functools.partial(_sel_body, tq=tq, tk=tk),
        grid_spec=grid_spec,
        out_shape=jax.ShapeDtypeStruct((t, NSA_HEADS * HEAD_DIM), F32),
        compiler_params=_cparams(("arbitrary", "arbitrary")),
        name="nsa_sel_attn",
    )(jnp.asarray(np.array(qi, np.int32)), jnp.asarray(np.array(ki, np.int32)), pb, mb, pb, pb, etab)


def _win_body(q_ref, k0_ref, k1_ref, k2_ref, v0_ref, v1_ref, v2_ref, oc_ref, os_ref, sm_ref, g_ref, o_ref, *, tq):
    i = pl.program_id(0)
    t_row = i * tq + (_iota((NSA_REP * tq, 1), 0) & (tq - 1))
    kp = jnp.concatenate([(i - 2 + j) * tq + _iota((1, tq), 1) for j in range(3)], axis=1)
    mask = (kp >= 0) & (kp <= t_row) & (kp > t_row - WINDOW)
    gates = jax.nn.sigmoid(sm_ref[...])
    outs = []
    for g in range(NSA_GROUPS):
        gs = slice(g * HEAD_DIM, (g + 1) * HEAD_DIM)
        qs = _stack_heads(q_ref[:, g * NSA_REP * HEAD_DIM:(g + 1) * NSA_REP * HEAD_DIM])
        k = jnp.concatenate([k0_ref[:, gs], k1_ref[:, gs], k2_ref[:, gs]], axis=0)
        v = jnp.concatenate([v0_ref[:, gs], v1_ref[:, gs], v2_ref[:, gs]], axis=0)
        p = _masked_softmax(_dot_nt(qs, k), mask)
        ow = _dot(p.astype(BF16), v)
        for h in range(NSA_REP):
            hh = g * NSA_REP + h
            hs = slice(hh * HEAD_DIM, (hh + 1) * HEAD_DIM)
            c = SM_GATE + 3 * hh
            o = (gates[:, c:c + 1] * oc_ref[:, hs] + gates[:, c + 1:c + 2] * os_ref[:, hs]
                 + gates[:, c + 2:c + 3] * ow[h * tq:(h + 1) * tq])
            o = o * lax.rsqrt(jnp.mean(o * o, -1, keepdims=True) + 1e-6) * g_ref[:, hs]
            outs.append(o)
    o_ref[...] = jnp.concatenate(outs, axis=1).astype(o_ref.dtype)


def _win_combine(pb, pf, o_cmp, o_sel, out_g):
    t = pb.shape[0]
    tq = 256
    wq = NSA_HEADS * HEAD_DIM
    wkv = NSA_GROUPS * HEAD_DIM

    def kvspec(j, unit):
        return pl.BlockSpec((tq, wkv), lambda i: (jnp.maximum(i - 2 + j, 0), unit // NSA_GROUPS))

    full = pl.BlockSpec((tq, wq), lambda i: (i, 0))
    return pl.pallas_call(
        functools.partial(_win_body, tq=tq),
        grid=(t // tq,),
        in_specs=[full,
                  kvspec(0, PB_KW), kvspec(1, PB_KW), kvspec(2, PB_KW),
                  kvspec(0, PB_VW), kvspec(1, PB_VW), kvspec(2, PB_VW),
                  full, full,
                  pl.BlockSpec((tq, LANES), lambda i: (i, PF_SMALL)),
                  pl.BlockSpec((1, wq), lambda i: (0, 0))],
        out_specs=full,
        out_shape=jax.ShapeDtypeStruct((t, wq), BF16),
        compiler_params=_cparams(("arbitrary",)),
        name="nsa_win_combine",
    )(pb, pb, pb, pb, pb, pb, pb, o_cmp, o_sel, pf, out_g)


def _gprep_body(x_ref, halo_ref, cw_ref, sm_ref, al_ref, dt_ref, qn_ref, kn_ref, v_ref, g_ref, b_ref, *, tm):
    i = pl.program_id(0)
    x = x_ref[...]
    halo = halo_ref[...] * (i > 0).astype(F32)
    xe = jnp.concatenate([halo, x], axis=0)
    cw = cw_ref[...]
    y = cw[GDN_CONV - 1:GDN_CONV] * x
    for j in range(GDN_CONV - 1):
        off = 8 - (GDN_CONV - 1) + j
        y = y + cw[j:j + 1] * xe[off:off + tm]
    y = y * jax.nn.sigmoid(y)
    nqk = GDN_QK_HEADS * HEAD_DIM
    for h in range(GDN_QK_HEADS):
        hs = slice(h * HEAD_DIM, (h + 1) * HEAD_DIM)
        q = y[:, hs]
        k = y[:, nqk + h * HEAD_DIM:nqk + (h + 1) * HEAD_DIM]
        qn_ref[:, hs] = q * lax.rsqrt(jnp.sum(q * q, -1, keepdims=True) + 1e-6)
        kn_ref[:, hs] = k * lax.rsqrt(jnp.sum(k * k, -1, keepdims=True) + 1e-6)
    v_ref[...] = y[:, 2 * nqk:]
    sm = sm_ref[...]
    b_ref[...] = jax.nn.sigmoid(sm)
    z = sm + dt_ref[...]
    softplus = jnp.maximum(z, 0.0) + jnp.log(1.0 + jnp.exp(-jnp.abs(z)))
    g_ref[...] = -jnp.exp(al_ref[...]) * softplus


def _gdn_prep(pf, conv_w, alog_pad, dt_pad):
    t = pf.shape[0]
    tm = 512
    cch = 2 * GDN_QK_HEADS * HEAD_DIM + GDN_V_HEADS * HEAD_DIM
    nqk = GDN_QK_HEADS * HEAD_DIM
    nv = GDN_V_HEADS * HEAD_DIM
    u = PF_QG * LANES // cch
    lane = pl.BlockSpec((1, LANES), lambda i: (0, 0))
    row = lambda w: pl.BlockSpec((tm, w), lambda i: (i, 0))
    return pl.pallas_call(
        functools.partial(_gprep_body, tm=tm),
        grid=(t // tm,),
        in_specs=[pl.BlockSpec((tm, cch), lambda i: (i, u)),
                  pl.BlockSpec((8, cch), lambda i: (jnp.maximum(i * (tm // 8) - 1, 0), u)),
                  pl.BlockSpec((GDN_CONV, cch), lambda i: (0, 0)),
                  pl.BlockSpec((tm, LANES), lambda i: (i, PF_SMALL)),
                  lane, lane],
        out_specs=[row(nqk), row(nqk), row(nv), row(LANES), row(LANES)],
        out_shape=[jax.ShapeDtypeStruct((t, nqk), F32), jax.ShapeDtypeStruct((t, nqk), F32),
                   jax.ShapeDtypeStruct((t, nv), F32), jax.ShapeDtypeStruct((t, LANES), F32),
                   jax.ShapeDtypeStruct((t, LANES), F32)],
        compiler_params=_cparams(("arbitrary",)),
        name="gdn_prep",
    )(pf, pf, conv_w, pf, alog_pad, dt_pad)


def _unit_lower_inverse(a, blk16):
    c = a.shape[0]
    eye = (_iota((c, c), 0) == _iota((c, c), 1)).astype(F32)
    ad = jnp.where(blk16, a, 0.0)
    ao = a - ad
    x2 = _dot3(ad, ad)
    x4 = _dot3(x2, x2)
    x8 = _dot3(x4, x4)
    p = eye - ad
    p = p + _dot3(p, x2)
    p = p + _dot3(p, x4)
    p = p + _dot3(p, x8)
    m = _dot3(p, ao)
    m2 = _dot3(m, m)
    t1 = p + _dot3(m2, p)
    return t1 - _dot3(m, t1)


def _gscan_body(q_ref, k_ref, v_ref, g_ref, b_ref, z_ref, ng_ref, o_ref, s_ref, *, chunks):
    c = GDN_CHUNK

    @pl.when(pl.program_id(0) == 0)
    def _():
        s_ref[...] = jnp.zeros(s_ref.shape, F32)

    r_i = _iota((c, c), 0)
    c_i = _iota((c, c), 1)
    tri = r_i >= c_i
    strict = r_i > c_i
    blk16 = (r_i >> 4) == (c_i >> 4)
    lower = tri.astype(BF16)
    upper = (r_i <= c_i).astype(BF16)
    ng = ng_ref[...]
    rep = GDN_V_HEADS // GDN_QK_HEADS

    def chunk(n, carry):
        rows = pl.ds(pl.multiple_of(n * c, c), c)
        gf = g_ref[rows, :]
        bf = b_ref[rows, :]
        gc_all = _dot_ones_l(lower, gf)
        gcr_all = _dot_ones_r(gf.T, upper)
        for h in range(GDN_V_HEADS):
            hq = h // rep
            qs_ = slice(hq * HEAD_DIM, (hq + 1) * HEAD_DIM)
            vs_ = slice(h * HEAD_DIM, (h + 1) * HEAD_DIM)
            gc = gc_all[:, SM_A + h:SM_A + h + 1]
            gcr = gcr_all[SM_A + h:SM_A + h + 1, :]
            beta = bf[:, SM_BETA + h:SM_BETA + h + 1]
            decay = jnp.where(tri, jnp.exp(jnp.where(tri, gc - gcr, 0.0)), 0.0)
            kh = k_ref[rows, qs_]
            qh = q_ref[rows, qs_] * (HEAD_DIM ** -0.5)
            kb = kh * beta
            khb = kh.astype(BF16)
            a = jnp.where(strict, _dot_nt(kb.astype(BF16), khb) * decay, 0.0)
            qk = jnp.where(tri, _dot_nt(qh.astype(BF16), khb) * decay, 0.0)
            egc = jnp.exp(gc)
            tinv = _unit_lower_inverse(a, blk16)
            uw = _dot3(tinv, jnp.concatenate([v_ref[rows, vs_] * beta, kb * egc], axis=1))
            u = uw[:, :HEAD_DIM]
            w = uw[:, HEAD_DIM:]
            g_last = gc[c - 1:c]
            k_dec = kh * jnp.exp(g_last - gc)
            st = s_ref[h]
            r = _dot(jnp.concatenate([w, qh * egc], axis=0).astype(BF16), st.astype(BF16))
            v_new = u - r[:c]
            o = r[c:] + _dot(qk.astype(BF16), v_new.astype(BF16))
            s_ref[h] = st * jnp.exp(g_last) + _dot(k_dec.T.astype(BF16), v_new.astype(BF16))
            zh = z_ref[rows, vs_]
            o = o * lax.rsqrt(jnp.mean(o * o, -1, keepdims=True) + 1e-6) * ng * (zh * jax.nn.sigmoid(zh))
            o_ref[rows, vs_] = o.astype(o_ref.dtype)
        return carry

    lax.fori_loop(0, chunks, chunk, 0)


def _gdn_scan(qn, kn, v, gfull, bfull, pf, norm_g):
    t = qn.shape[0]
    chunks = 4
    tm = chunks * GDN_CHUNK
    nqk = GDN_QK_HEADS * HEAD_DIM
    nv = GDN_V_HEADS * HEAD_DIM
    row = lambda w: pl.BlockSpec((tm, w), lambda i: (i, 0))
    return pl.pallas_call(
        functools.partial(_gscan_body, chunks=chunks),
        grid=(t // tm,),
        in_specs=[row(nqk), row(nqk), row(nv), row(LANES), row(LANES),
                  pl.BlockSpec((tm, nv), lambda i: (i, PF_Z * LANES // nv)),
                  pl.BlockSpec((1, HEAD_DIM), lambda i: (0, 0))],
        out_specs=row(nv),
        out_shape=jax.ShapeDtypeStruct((t, nv), BF16),
        scratch_shapes=[pltpu.VMEM((GDN_V_HEADS, HEAD_DIM, HEAD_DIM), F32)],
        compiler_params=_cparams(("arbitrary",)),
        name="gdn_scan",
    )(qn, kn, v, gfull, bfull, pf, norm_g)


def _post_body(on_ref, og_ref, w_ref, x_ref, gm_ref, lg_ref, lb_ref, sh_ref, sc_ref, rw_ref, rb_ref, tril_ref,
               x1_ref, hp_ref, ti_ref, gt_ref, rk_ref, cnt_ref, carry_ref):
    half = D_MODEL // 2

    @pl.when(pl.program_id(0) == 0)
    def _():
        carry_ref[...] = jnp.zeros(carry_ref.shape, F32)

    mix = _dot(on_ref[...], w_ref[0:half, :]) + _dot(og_ref[...], w_ref[half:, :])
    x1 = _ln_plain(DEEPNORM_ALPHA * x_ref[...] + gm_ref[...] * mix, 1e-5) * lg_ref[...] + lb_ref[...]
    x1_ref[...] = x1
    h2 = _ln_plain(x1, 1e-6) * (1.0 + sc_ref[...]) + sh_ref[...]
    lo = lax.bitcast_convert_type(h2[:, :half].astype(BF16).astype(F32), U32) >> 16
    hi = lax.bitcast_convert_type(h2[:, half:].astype(BF16).astype(F32), U32) & jnp.uint32(0xFFFF0000)
    hp_ref[...] = lo | hi
    sc = _dot3(h2, rw_ref[...]) + rb_ref[...]
    lane = _iota((1, LANES), 1).astype(F32)
    hot = jnp.zeros(sc.shape, F32)
    vals, idxs = [], []
    for _ in range(TOP_K):
        m = jnp.max(sc, -1, keepdims=True)
        idx = jnp.min(jnp.where(sc == m, lane, float(LANES)), -1, keepdims=True)
        pick = lane == idx
        hot = jnp.where(pick, 1.0, hot)
        sc = jnp.where(pick, -jnp.inf, sc)
        vals.append(m)
        idxs.append(idx)
    ex = [jnp.exp(v - vals[0]) for v in vals]
    den = (ex[0] + ex[1]) + (ex[2] + ex[3])
    before = carry_ref[...] + _dot(tril_ref[...], hot.astype(BF16))
    ti = jnp.zeros(sc.shape, F32)
    gt = jnp.zeros(sc.shape, F32)
    rk = jnp.zeros(sc.shape, F32)
    for k in range(TOP_K):
        at_k = lane == float(k)
        ti = jnp.where(at_k, idxs[k], ti)
        gt = jnp.where(at_k, ex[k] / den, gt)
        rk = jnp.where(at_k, jnp.sum(jnp.where(lane == idxs[k], before, 0.0), -1, keepdims=True), rk)
    ti_ref[...] = ti.astype(I32)
    gt_ref[...] = gt
    rk_ref[...] = rk.astype(I32)
    carry_ref[...] = carry_ref[...] + jnp.sum(hot, axis=0, keepdims=True)
    cnt_ref[...] = carry_ref[...]


def _post_mixer(o_nsa, o_gdn, w_out, x, gm1, ln_g, ln_b, sh_f, sc_f, rw_pad, rb_pad, tril):
    t, d = x.shape
    tm = tril.shape[0]
    half = d // 2
    vec = pl.BlockSpec((1, d), lambda i: (0, 0))
    lane_vec = pl.BlockSpec((1, LANES), lambda i: (0, 0))
    row = lambda w: pl.BlockSpec((tm, w), lambda i: (i, 0))
    return pl.pallas_call(
        _post_body,
        grid=(t // tm,),
        in_specs=[row(half), row(half), pl.BlockSpec((d, d), lambda i: (0, 0)), row(d),
                  vec, vec, vec, vec, vec,
                  pl.BlockSpec((d, LANES), lambda i: (0, 0)), lane_vec,
                  pl.BlockSpec((tm, tm), lambda i: (0, 0))],
        out_specs=[row(d), row(half), row(LANES), row(LANES), row(LANES), lane_vec],
        out_shape=[jax.ShapeDtypeStruct((t, d), F32), jax.ShapeDtypeStruct((t, half), U32),
                   jax.ShapeDtypeStruct((t, LANES), I32), jax.ShapeDtypeStruct((t, LANES), F32),
                   jax.ShapeDtypeStruct((t, LANES), I32), jax.ShapeDtypeStruct((1, LANES), F32)],
        scratch_shapes=[pltpu.VMEM((1, LANES), F32)],
        compiler_params=_cparams(("arbitrary",)),
        name="post_mixer_router",
    )(o_nsa, o_gdn, w_out, x, gm1, ln_g, ln_b, sh_f, sc_f, rw_pad, rb_pad, tril)


def _row_copy(src_ref, dst_ref, sem):
    return pltpu.make_async_copy(src_ref, dst_ref, sem)


def _dispatch_body(dest_ref, hp_ref, xs_in_ref, xs_ref, sem, *, tm):
    del xs_in_ref

    def issue(r, carry):
        for k in range(TOP_K):
            _row_copy(hp_ref.at[pl.ds(r, 1)], xs_ref.at[pl.ds(dest_ref[r * TOP_K + k], 1)], sem).start()
        return carry

    def drain(r, carry):
        for k in range(TOP_K):
            _row_copy(hp_ref.at[pl.ds(0, 1)], xs_ref.at[pl.ds(0, 1)], sem).wait()
        return carry

    lax.fori_loop(0, tm, issue, 0)
    lax.fori_loop(0, tm, drain, 0)


def _dispatch(dest, hp, rows):
    t, w = hp.shape
    tm = 256
    zeros = jnp.zeros((rows, w), hp.dtype)
    return pl.pallas_call(
        functools.partial(_dispatch_body, tm=tm),
        grid=(t // tm,),
        in_specs=[pl.BlockSpec((tm * TOP_K,), lambda i: (i,), memory_space=pltpu.SMEM),
                  pl.BlockSpec((tm, w), lambda i: (i, 0)),
                  pl.BlockSpec(memory_space=pl.ANY)],
        out_specs=pl.BlockSpec(memory_space=pl.ANY),
        out_shape=jax.ShapeDtypeStruct((rows, w), hp.dtype),
        scratch_shapes=[pltpu.SemaphoreType.DMA(())],
        input_output_aliases={2: 0},
        compiler_params=_cparams(("arbitrary",)),
        name="moe_dispatch",
    )(dest, hp, zeros)


def _gu_body(be_ref, xs_ref, wg_ref, wl_ref, bg_ref, bl_ref, o_ref):
    del be_ref
    half = D_MODEL // 2
    u = xs_ref[...]
    xa = lax.bitcast_convert_type(u << 16, F32).astype(BF16)
    xb = lax.bitcast_convert_type(u & jnp.uint32(0xFFFF0000), F32).astype(BF16)
    tn = 512
    for c in range(D_FF // tn):
        cs = slice(c * tn, (c + 1) * tn)
        hg = _dot(xa, wg_ref[0, 0:half, cs]) + _dot(xb, wg_ref[0, half:, cs]) + bg_ref[0, :, cs]
        hl = _dot(xa, wl_ref[0, 0:half, cs]) + _dot(xb, wl_ref[0, half:, cs]) + bl_ref[0, :, cs]
        x_glu = jnp.minimum(hg, SWIGLU_LIMIT)
        x_lin = jnp.clip(hl, -SWIGLU_LIMIT, SWIGLU_LIMIT)
        o_ref[:, cs] = (x_glu * jax.nn.sigmoid(SWIGLU_ALPHA * x_glu) * (x_lin + 1.0)).astype(o_ref.dtype)


def _expert_gu(blk_exp, xs, wg, wl, bg, bl):
    rows, half = xs.shape
    nb = rows // EXPERT_ROWS
    wspec = pl.BlockSpec((1, D_MODEL, D_FF), lambda b, be: (be[b], 0, 0))
    bspec = pl.BlockSpec((1, 1, D_FF), lambda b, be: (be[b], 0, 0))
    grid_spec = pltpu.PrefetchScalarGridSpec(
        num_scalar_prefetch=1,
        grid=(nb,),
        in_specs=[pl.BlockSpec((EXPERT_ROWS, half), lambda b, be: (b, 0)), wspec, wspec, bspec, bspec],
        out_specs=pl.BlockSpec((EXPERT_ROWS, D_FF), lambda b, be: (b, 0)),
    )
    return pl.pallas_call(
        _gu_body,
        grid_spec=grid_spec,
        out_shape=jax.ShapeDtypeStruct((rows, D_FF), BF16),
        compiler_params=_cparams(("arbitrary",)),
        name="moe_gate_up",
    )(blk_exp, xs, wg, wl, bg, bl)


def _dn_body(be_ref, h_ref, w_ref, b_ref, o_ref):
    del be_ref
    o_ref[...] = _dot(h_ref[...], w_ref[0]) + b_ref[0]


def _expert_dn(blk_exp, hact, wd, bd):
    rows = hact.shape[0]
    nb = rows // EXPERT_ROWS
    grid_spec = pltpu.PrefetchScalarGridSpec(
        num_scalar_prefetch=1,
        grid=(nb,),
        in_specs=[pl.BlockSpec((EXPERT_ROWS, D_FF), lambda b, be: (b, 0)),
                  pl.BlockSpec((1, D_FF, D_MODEL), lambda b, be: (be[b], 0, 0)),
                  pl.BlockSpec((1, 1, D_MODEL), lambda b, be: (be[b], 0, 0))],
        out_specs=pl.BlockSpec((EXPERT_ROWS, D_MODEL), lambda b, be: (b, 0)),
    )
    return pl.pallas_call(
        _dn_body,
        grid_spec=grid_spec,
        out_shape=jax.ShapeDtypeStruct((rows, D_MODEL), F32),
        compiler_params=_cparams(("arbitrary",)),
        name="moe_down",
    )(blk_exp, hact, wd, bd)


def _combine_body(dest_ref, y_ref, gt_ref, x1_ref, gf_ref, lg_ref, lb_ref, o_ref, buf_ref, sem, *, tm):
    def issue(r, carry):
        for k in range(TOP_K):
            _row_copy(y_ref.at[pl.ds(dest_ref[r * TOP_K + k], 1)], buf_ref.at[k, pl.ds(r, 1)], sem).start()
        return carry

    def drain(r, carry):
        for k in range(TOP_K):
            _row_copy(y_ref.at[pl.ds(0, 1)], buf_ref.at[k, pl.ds(0, 1)], sem).wait()
        return carry

    lax.fori_loop(0, tm, issue, 0)
    lax.fori_loop(0, tm, drain, 0)
    gt = gt_ref[...]
    y = (gt[:, 0:1] * buf_ref[0] + gt[:, 1:2] * buf_ref[1]) + (gt[:, 2:3] * buf_ref[2] + gt[:, 3:4] * buf_ref[3])
    o_ref[...] = _ln_plain(DEEPNORM_ALPHA * x1_ref[...] + gf_ref[...] * y, 1e-5) * lg_ref[...] + lb_ref[...]


def _combine(dest, y, gt, x1, gf1, ln_g, ln_b):
    t, d = x1.shape
    tm = 256
    vec = pl.BlockSpec((1, d), lambda i: (0, 0))
    return pl.pallas_call(
        functools.partial(_combine_body, tm=tm),
        grid=(t // tm,),
        in_specs=[pl.BlockSpec((tm * TOP_K,), lambda i: (i,), memory_space=pltpu.SMEM),
                  pl.BlockSpec(memory_space=pl.ANY),
                  pl.BlockSpec((tm, LANES), lambda i: (i, 0)),
                  pl.BlockSpec((tm, d), lambda i: (i, 0)), vec, vec, vec],
        out_specs=pl.BlockSpec((tm, d), lambda i: (i, 0)),
        out_shape=jax.ShapeDtypeStruct((t, d), F32),
        scratch_shapes=[pltpu.VMEM((TOP_K, tm, d), F32), pltpu.SemaphoreType.DMA(())],
        compiler_params=_cparams(("arbitrary",)),
        name="moe_combine",
    )(dest, y, gt, x1, gf1, ln_g, ln_b)


def _overlap_matrix(t):
    nc_pad = t // CMP_STRIDE
    n_c = (t - CMP_BLOCK) // CMP_STRIDE + 1
    n_s = t // SEL_BLOCK
    nsp = -(-n_s // LANES) * LANES
    ci = np.arange(nc_pad)[:, None] * CMP_STRIDE
    sj = np.arange(nsp)[None, :] * SEL_BLOCK
    ov = (ci < sj + SEL_BLOCK) & (ci + CMP_BLOCK > sj)
    ov &= (np.arange(nc_pad)[:, None] < n_c) & (np.arange(nsp)[None, :] < n_s)
    return jnp.asarray(ov.astype(np.float32), dtype=BF16)


def _block_onehot():
    e = (np.arange(SEL_SUPER)[:, None] // SEL_BLOCK) == np.arange(LANES)[None, :]
    return jnp.asarray(e.astype(np.float32), dtype=BF16)


def _in_proj_weights(w_in):
    nq = NSA_HEADS * HEAD_DIM
    nkv = NSA_GROUPS * HEAD_DIM
    o = np.cumsum([0, nq, nkv, nkv, nkv, nkv, nkv, nkv, 3 * NSA_HEADS,
                   GDN_QK_HEADS * HEAD_DIM, GDN_QK_HEADS * HEAD_DIM, GDN_V_HEADS * HEAD_DIM,
                   GDN_V_HEADS * HEAD_DIM, GDN_V_HEADS, GDN_V_HEADS]).tolist()
    q, kc, vc, ks, vs, kw, vw, gate, qg, kg, vg, z, b_l, a_l = [w_in[:, o[i]:o[i + 1]] for i in range(14)]
    wb = jnp.concatenate([q, ks, vs, kw, vw], axis=1).astype(BF16)
    small = jnp.concatenate([gate, b_l, a_l], axis=1)
    pad = jnp.zeros((w_in.shape[0], 4 * LANES - small.shape[1]), w_in.dtype)
    wf = jnp.concatenate([qg, kg, vg, z, kc, vc, small, pad], axis=1).astype(BF16)
    cs_b = jnp.concatenate([jnp.full((1, nq), HEAD_DIM ** -0.5, F32), jnp.ones((1, wb.shape[1] - nq), F32)], axis=1)
    cs_f = jnp.ones((1, wf.shape[1]), F32)
    return wb, cs_b, wf, cs_f


def _lane_pad(v, start):
    return jnp.zeros((1, LANES), F32).at[0, start:start + v.shape[0]].set(v)


def _layer(x, mod, w_in, cmp_pos, cmp_w1, cmp_w2, nsa_out_g, conv_w, a_log, dt_bias, gdn_norm_g, w_out,
           ln_mix_g, ln_mix_b, router_w, router_b, w_gu, b_gu, w_dn, b_dn, ln_ffn_g, ln_ffn_b, consts):
    t, d = x.shape
    ov, etab, tril = consts
    sh_m, sc_m, g_m, sh_f, sc_f, g_f = [mod[:, i * d:(i + 1) * d] for i in range(6)]

    h = _lnmod(x, sh_m, sc_m)
    wb, cs_b, wf, cs_f = _in_proj_weights(w_in)
    pb = _matmul(h, wb, cs_b, BF16)
    pf = _matmul(h, wf, cs_f, F32)
    nc = t // CMP_STRIDE
    x4 = jnp.stack([pf[:, u * LANES:(u + 1) * LANES].reshape(nc, CMP_STRIDE * HEAD_DIM)
                    for u in range(PF_KC, PF_KC + 2 * NSA_GROUPS)])
    cc = _compress(x4, cmp_pos, cmp_w1, cmp_w2)
    n_sel = min(SEL_COUNT, t // SEL_BLOCK)
    o_cmp, mb = _cmp_select(pb, cc, ov, n_sel)
    o_sel = _sel_attention(pb, mb, etab)
    o_nsa = _win_combine(pb, pf, o_cmp, o_sel, nsa_out_g.reshape(1, -1))
    qn, kn, v, gfull, bfull = _gdn_prep(pf, conv_w.reshape(GDN_CONV, -1), _lane_pad(a_log, SM_A),
                                        _lane_pad(dt_bias, SM_A))
    o_gdn = _gdn_scan(qn, kn, v, gfull, bfull, pf, gdn_norm_g.reshape(1, -1))

    rw_pad = jnp.zeros((d, LANES), F32).at[:, :N_EXPERTS].set(router_w)
    rb_pad = jnp.full((1, LANES), -jnp.inf, F32).at[0, :N_EXPERTS].set(router_b)
    x1, hp, ti, gt, rk, cnt = _post_mixer(o_nsa, o_gdn, w_out.astype(BF16), x, 1.0 + g_m, ln_mix_g.reshape(1, d),
                                           ln_mix_b.reshape(1, d), sh_f, sc_f, rw_pad, rb_pad, tril)

    counts = cnt[0, :N_EXPERTS].astype(I32)
    padded = (counts + EXPERT_ROWS - 1) // EXPERT_ROWS * EXPERT_ROWS
    pad_end = jnp.cumsum(padded)
    pad_start = pad_end - padded
    dest = (pad_start[ti[:, :TOP_K]] + rk[:, :TOP_K]).reshape(-1)
    nb = t * TOP_K // EXPERT_ROWS + N_EXPERTS
    blk_exp = jnp.minimum(jnp.searchsorted(pad_end, jnp.arange(nb, dtype=I32) * EXPERT_ROWS, side="right"),
                          N_EXPERTS - 1).astype(I32)

    xs = _dispatch(dest, hp, nb * EXPERT_ROWS)
    wg = w_gu[:, :, 0::2].astype(BF16)
    wl = w_gu[:, :, 1::2].astype(BF16)
    bg = b_gu[:, None, 0::2]
    bl = b_gu[:, None, 1::2]
    hact = _expert_gu(blk_exp, xs, wg, wl, bg, bl)
    y = _expert_dn(blk_exp, hact, w_dn.astype(BF16), b_dn[:, None, :])
    return _combine(dest, y, gt, x1, 1.0 + g_f, ln_ffn_g.reshape(1, d), ln_ffn_b.reshape(1, d))


def kernel(x, c, w_in, cmp_pos_k, cmp_pos_v, cmp_k_w1, cmp_k_w2, cmp_v_w1, cmp_v_w2, nsa_out_g, gdn_conv_w,
           gdn_a_log, gdn_dt_bias, gdn_norm_g, w_out, w_ada, b_ada, ln_mix_g, ln_mix_b, router_w, router_b,
           w_gu, b_gu, w_dn, b_dn, ln_ffn_g, ln_ffn_b):
    bsz, t, d = x.shape
    assert bsz == 1 and d == D_MODEL and t % 1024 == 0
    nl = w_in.shape[0]
    consts = (_overlap_matrix(t), _block_onehot(),
              jnp.asarray(np.tril(np.ones((512, 512), np.float32), -1), dtype=BF16))
    mod = _ada(c, w_ada, b_ada)
    half_w = CMP_STRIDE * HEAD_DIM
    xt = x.reshape(t, d)
    for l in range(nl):
        pos = jnp.stack([cmp_pos_k[l], cmp_pos_v[l]]).reshape(2, 2, 1, half_w)
        w1 = jnp.stack([cmp_k_w1[l], cmp_v_w1[l]]).reshape(2, 2, half_w, CMP_HIDDEN).astype(BF16)
        w2 = jnp.stack([cmp_k_w2[l], cmp_v_w2[l]]).astype(BF16)
        xt = _layer(xt, mod[l], w_in[l], pos, w1, w2, nsa_out_g[l], gdn_conv_w[l], gdn_a_log[l], gdn_dt_bias[l],
                    gdn_norm_g[l], w_out[l], ln_mix_g[l], ln_mix_b[l], router_w[l], router_b[l], w_gu[l], b_gu[l],
                    w_dn[l], b_dn[l], ln_ffn_g[l], ln_ffn_b[l], consts)
    return xt.reshape(bsz, t, d)
```

```python
import functools

import numpy as np
import jax
import jax.numpy as jnp
from jax import lax
from jax.experimental import pallas as pl
from jax.experimental.pallas import tpu as pltpu

F32 = jnp.float32
BF16 = jnp.bfloat16
I32 = jnp.int32
U32 = jnp.uint32

D_MODEL = 2048
DEPTH = 2
HEAD_DIM = 128
NSA_HEADS = 8
NSA_GROUPS = 2
NSA_REP = NSA_HEADS // NSA_GROUPS
CMP_BLOCK = 32
CMP_STRIDE = 16
CMP_HIDDEN = 256
SEL_BLOCK = 64
SEL_COUNT = 16
WINDOW = 512
GDN_QK_HEADS = 4
GDN_V_HEADS = 8
GDN_CONV = 4
GDN_CHUNK = 64
N_EXPERTS = 32
TOP_K = 4
D_FF = 2048
SWIGLU_LIMIT = 7.0
SWIGLU_ALPHA = 1.702
DEEPNORM_ALPHA = (2 * DEPTH) ** 0.25

LANES = 128
VMEM_LIMIT = 56 * 1024 * 1024
NEG = -1e30
LOG2E = 1.4426950408889634
SEL_SUPER = LANES * SEL_BLOCK
EXPERT_ROWS = 512

PB_Q, PB_KS, PB_VS, PB_KW, PB_VW = 0, 8, 10, 12, 14
PF_QG, PF_KG, PF_VG, PF_Z, PF_KC, PF_VC, PF_SMALL = 0, 4, 8, 16, 24, 26, 28
SM_GATE, SM_BETA, SM_A = 0, 24, 32


def _cparams(sem):
    return pltpu.CompilerParams(dimension_semantics=sem, vmem_limit_bytes=VMEM_LIMIT)


def _dot(a, b):
    return jnp.dot(a, b, preferred_element_type=F32)


def _dot_nt(a, b):
    return lax.dot_general(a, b, (((1,), (1,)), ((), ())), preferred_element_type=F32)


def _split2(a):
    hi = a.astype(BF16)
    return hi, (a - hi.astype(F32)).astype(BF16)


def _split3(a):
    hi = a.astype(BF16)
    r = a - hi.astype(F32)
    mid = r.astype(BF16)
    return hi, mid, (r - mid.astype(F32)).astype(BF16)


def _dot3(a, b):
    ah, al = _split2(a)
    bh, bl = _split2(b)
    return _dot(ah, bh) + (_dot(ah, bl) + _dot(al, bh))


def _dot_ones_l(ones_bf16, x):
    h, m, l = _split3(x)
    return _dot(ones_bf16, h) + (_dot(ones_bf16, m) + _dot(ones_bf16, l))


def _dot_ones_r(x, ones_bf16):
    h, m, l = _split3(x)
    return _dot(h, ones_bf16) + (_dot(m, ones_bf16) + _dot(l, ones_bf16))


def _ln_plain(x, eps):
    mu = jnp.mean(x, -1, keepdims=True)
    xc = x - mu
    var = jnp.mean(xc * xc, -1, keepdims=True)
    return xc * lax.rsqrt(var + eps)


def _masked_softmax2(s2, mask):
    s2 = jnp.where(mask, s2, -jnp.inf)
    m = jnp.max(s2, -1, keepdims=True)
    m = jnp.where(jnp.abs(m) < jnp.inf, m, 0.0)
    p = jnp.exp2(s2 - m)
    return p / jnp.maximum(jnp.sum(p, -1, keepdims=True), 1e-30)


def _iota(shape, dim):
    return lax.broadcasted_iota(I32, shape, dim)


def _stack_heads(q):
    return jnp.concatenate([q[:, h * HEAD_DIM:(h + 1) * HEAD_DIM] for h in range(NSA_REP)], axis=0)


def _unstack_heads(o, t):
    return jnp.concatenate([o[h * t:(h + 1) * t] for h in range(NSA_REP)], axis=1)


def _ada_body(c_ref, w_ref, b_ref, o_ref):
    c = c_ref[...]
    ca = c * jax.nn.sigmoid(c)
    o_ref[0] = jnp.sum(w_ref[0] * ca, axis=0, keepdims=True) + b_ref[0]


def _ada(c, w_ada, b_ada):
    nl, d, n = w_ada.shape
    tn = 1024
    return pl.pallas_call(
        _ada_body,
        grid=(nl, n // tn),
        in_specs=[pl.BlockSpec((d, 1), lambda l, j: (0, 0)),
                  pl.BlockSpec((1, d, tn), lambda l, j: (l, 0, j)),
                  pl.BlockSpec((1, 1, tn), lambda l, j: (l, 0, j))],
        out_specs=pl.BlockSpec((1, 1, tn), lambda l, j: (l, 0, j)),
        out_shape=jax.ShapeDtypeStruct((nl, 1, n), F32),
        compiler_params=_cparams(("arbitrary", "arbitrary")),
        name="ada_mod",
    )(c.reshape(d, 1), w_ada, b_ada.reshape(nl, 1, n))


def _lnmod_body(x_ref, sh_ref, sc_ref, o_ref):
    h = _ln_plain(x_ref[...], 1e-6) * (1.0 + sc_ref[...]) + sh_ref[...]
    o_ref[...] = h.astype(o_ref.dtype)


def _lnmod(x, shift, scale):
    t, d = x.shape
    tm = 512
    vec = pl.BlockSpec((1, d), lambda i: (0, 0))
    return pl.pallas_call(
        _lnmod_body,
        grid=(t // tm,),
        in_specs=[pl.BlockSpec((tm, d), lambda i: (i, 0)), vec, vec],
        out_specs=pl.BlockSpec((tm, d), lambda i: (i, 0)),
        out_shape=jax.ShapeDtypeStruct((t, d), BF16),
        compiler_params=_cparams(("arbitrary",)),
        name="ln_mod",
    )(x, shift, scale)


def _mm_body(a_ref, w_ref, cs_ref, o_ref):
    o_ref[...] = (_dot(a_ref[...], w_ref[...]) * cs_ref[...]).astype(o_ref.dtype)


def _matmul(a, w, colscale, out_dtype):
    m, k = a.shape
    n = w.shape[1]
    tm, tn = 1024, 512
    return pl.pallas_call(
        _mm_body,
        grid=(m // tm, n // tn),
        in_specs=[pl.BlockSpec((tm, k), lambda i, j: (i, 0)),
                  pl.BlockSpec((k, tn), lambda i, j: (0, j)),
                  pl.BlockSpec((1, tn), lambda i, j: (0, j))],
        out_specs=pl.BlockSpec((tm, tn), lambda i, j: (i, j)),
        out_shape=jax.ShapeDtypeStruct((m, n), out_dtype),
        compiler_params=_cparams(("arbitrary", "arbitrary")),
        name="proj_mm",
    )(a, w, colscale)


def _compress_body(x_ref, p_ref, w1_ref, w2_ref, o_ref):
    x = x_ref[0]
    nc = x.shape[0]
    y = _dot((x + p_ref[0, 0]).astype(BF16), w1_ref[0, 0])
    z = _dot((x + p_ref[0, 1]).astype(BF16), w1_ref[0, 1])
    z_next = pltpu.roll(z, nc - 1, 0)
    hid = jax.nn.gelu(y + z_next)
    o_ref[0] = _dot(hid.astype(BF16), w2_ref[0])


def _compress(x4, pos, w1, w2):
    n4, nc, wdt = x4.shape
    return pl.pallas_call(
        _compress_body,
        grid=(n4,),
        in_specs=[pl.BlockSpec((1, nc, wdt), lambda i: (i, 0, 0)),
                  pl.BlockSpec((1, 2, 1, wdt), lambda i: (i // NSA_GROUPS, 0, 0, 0)),
                  pl.BlockSpec((1, 2, wdt, CMP_HIDDEN), lambda i: (i // NSA_GROUPS, 0, 0, 0)),
                  pl.BlockSpec((1, CMP_HIDDEN, HEAD_DIM), lambda i: (i // NSA_GROUPS, 0, 0))],
        out_specs=pl.BlockSpec((1, nc, HEAD_DIM), lambda i: (i, 0, 0)),
        out_shape=jax.ShapeDtypeStruct((n4, nc, HEAD_DIM), F32),
        compiler_params=_cparams(("arbitrary",)),
        name="nsa_compress",
    )(x4, pos, w1, w2)


def _cmp_body(q_ref, kc_ref, vc_ref, ov_ref, o_ref, mb_ref, *, tq, n_sel):
    i = pl.program_id(1)
    nc = kc_ref.shape[1]
    nsp = ov_ref.shape[1]
    qs = _stack_heads(q_ref[...])
    s = _dot_nt(qs, kc_ref[0].astype(BF16))
    t_row = i * tq + (_iota((NSA_REP * tq, 1), 0) & (tq - 1))
    cmp_end = _iota((1, nc), 1) * CMP_STRIDE + (CMP_BLOCK - 1)
    p = _masked_softmax2(s, cmp_end <= t_row)
    o = _dot(p.astype(BF16), vc_ref[0].astype(BF16))
    o_ref[...] = _unstack_heads(o, tq)
    ps = (p[0:tq] + p[tq:2 * tq]) + (p[2 * tq:3 * tq] + p[3 * tq:4 * tq])
    hi, lo = _split2(ps)
    ov = ov_ref[...]
    imp = _dot(hi, ov) + _dot(lo, ov)
    cur = (i * tq + _iota((tq, 1), 0)) >> 6
    blk = _iota((1, nsp), 1)
    valid = blk <= cur
    forced = (blk == 0) | (blk == cur) | (blk == cur - 1)
    score = jnp.where(valid, imp + jnp.where(forced, jnp.inf, 0.0), -jnp.inf)
    blkf = blk.astype(F32)
    sel = jnp.zeros((tq, nsp), F32)
    for _ in range(n_sel):
        m = jnp.max(score, -1, keepdims=True)
        idx = jnp.min(jnp.where(score == m, blkf, float(nsp)), -1, keepdims=True)
        pick = blkf == idx
        sel = jnp.where(pick & (m > -jnp.inf), 1.0, sel)
        score = jnp.where(pick, -jnp.inf, score)
    mb_ref[0] = jnp.where(sel > 0.0, 0.0, NEG).astype(BF16)


def _cmp_select(pb, cc, ov, n_sel):
    t = pb.shape[0]
    nc, nsp = ov.shape
    tq = 128
    return pl.pallas_call(
        functools.partial(_cmp_body, tq=tq, n_sel=n_sel),
        grid=(NSA_GROUPS, t // tq),
        in_specs=[pl.BlockSpec((tq, NSA_REP * HEAD_DIM), lambda g, i: (i, g)),
                  pl.BlockSpec((1, nc, HEAD_DIM), lambda g, i: (g, 0, 0)),
                  pl.BlockSpec((1, nc, HEAD_DIM), lambda g, i: (NSA_GROUPS + g, 0, 0)),
                  pl.BlockSpec((nc, nsp), lambda g, i: (0, 0))],
        out_specs=[pl.BlockSpec((tq, NSA_REP * HEAD_DIM), lambda g, i: (i, g)),
                   pl.BlockSpec((1, tq, nsp), lambda g, i: (g, i, 0))],
        out_shape=[jax.ShapeDtypeStruct((t, NSA_HEADS * HEAD_DIM), F32),
                   jax.ShapeDtypeStruct((NSA_GROUPS, t, nsp), BF16)],
        compiler_params=_cparams(("arbitrary", "arbitrary")),
        name="nsa_cmp_select",
    )(pb, cc, cc, ov)


def _sel_body(qi_ref, ki_ref, qt_ref, mbt_ref, k_ref, e_ref, vt_ref, o_ref, wq_ref, m_ref, acc_ref, *, tq, tk):
    step = pl.program_id(1)
    qi = qi_ref[step]
    ki = ki_ref[step]

    @pl.when(ki == 0)
    def _():
        for h in range(NSA_REP):
            wq_ref[h, 0:HEAD_DIM, :] = qt_ref[h * HEAD_DIM:(h + 1) * HEAD_DIM, :]
        m_ref[...] = jnp.full(m_ref.shape, NEG, F32)
        acc_ref[...] = jnp.zeros(acc_ref.shape, F32)

    @pl.when((ki & (SEL_SUPER // tk - 1)) == 0)
    def _():
        mbt = mbt_ref[0]
        for h in range(NSA_REP):
            wq_ref[h, HEAD_DIM:2 * HEAD_DIM, :] = mbt

    kaug = jnp.concatenate([k_ref[...], e_ref[...]], axis=1)
    vaug = vt_ref[0]

    def update(causal):
        if causal:
            keep = (ki * tk + _iota((tk, 1), 0)) <= (qi * tq + _iota((1, tq), 1))
        for h in range(NSA_REP):
            s = _dot(kaug, wq_ref[h])
            if causal:
                s = jnp.where(keep, s, NEG)
            m_prev = m_ref[h]
            m_new = jnp.maximum(m_prev, jnp.max(s, axis=0, keepdims=True))
            p = jnp.exp2(s - m_new)
            acc_ref[h] = jnp.exp2(m_prev - m_new) * acc_ref[h] + _dot(vaug, p.astype(BF16))
            m_ref[h] = m_new

    crosses_diagonal = ki * tk + (tk - 1) > qi * tq
    pl.when(crosses_diagonal)(functools.partial(update, True))
    pl.when(jnp.logical_not(crosses_diagonal))(functools.partial(update, False))

    @pl.when(ki == ((qi + 1) * tq - 1) // tk)
    def _():
        for h in range(NSA_REP):
            a = acc_ref[h]
            o_ref[:, h * HEAD_DIM:(h + 1) * HEAD_DIM] = (a[0:HEAD_DIM] / a[HEAD_DIM:HEAD_DIM + 1]).T


def _sel_attention(pb, mb, etab):
    t = pb.shape[0]
    tq, tk = 512, 512
    qi, ki = [], []
    for a in range(t // tq):
        for b in range(((a + 1) * tq - 1) // tk + 1):
            qi.append(a)
            ki.append(b)
    nsteps = len(qi)
    sup_tiles = SEL_SUPER // tk
    wq = NSA_HEADS * HEAD_DIM
    vrows = HEAD_DIM + 16
    qt = pb[:, PB_Q * LANES:PB_Q * LANES + wq].T
    mbt = jnp.swapaxes(mb, 1, 2)
    vt = jnp.stack([jnp.concatenate([pb[:, (PB_VS + g) * LANES:(PB_VS + g + 1) * LANES].T,
                                     jnp.ones((vrows - HEAD_DIM, t), pb.dtype)], axis=0)
                    for g in range(NSA_GROUPS)])
    grid_spec = pltpu.PrefetchScalarGridSpec(
        num_scalar_prefetch=2,
        grid=(NSA_GROUPS, nsteps),
        in_specs=[pl.BlockSpec((NSA_REP * HEAD_DIM, tq), lambda g, s, qi, ki: (g, qi[s])),
                  pl.BlockSpec((1, LANES, tq), lambda g, s, qi, ki: (g, ki[s] // sup_tiles, qi[s])),
                  pl.BlockSpec((tk, HEAD_DIM), lambda g, s, qi, ki: (ki[s], PB_KS + g)),
                  pl.BlockSpec((tk, LANES), lambda g, s, qi, ki: (ki[s] % sup_tiles, 0)),
                  pl.BlockSpec((1, vrows, tk), lambda g, s, qi, ki: (g, 0, ki[s]))],
        out_specs=pl.BlockSpec((tq, NSA_REP * HEAD_DIM), lambda g, s, qi, ki: (qi[s], g)),
        scratch_shapes=[pltpu.VMEM((NSA_REP, 2 * HEAD_DIM, tq), BF16),
                        pltpu.VMEM((NSA_REP, 1, tq), F32),
                        pltpu.VMEM((NSA_REP, vrows, tq), F32)],
    )
    return pl.pallas_call(
        functools.partial(_sel_body, tq=tq, tk=tk),
        grid_spec=grid_spec,
        out_shape=jax.ShapeDtypeStruct((t, wq), F32),
        compiler_params=_cparams(("arbitrary", "arbitrary")),
        name="nsa_sel_attn",
    )(jnp.asarray(np.array(qi, np.int32)), jnp.asarray(np.array(ki, np.int32)), qt, mbt, pb, etab, vt)


def _win_body(q_ref, k0_ref, k1_ref, k2_ref, v0_ref, v1_ref, v2_ref, oc_ref, os_ref, sm_ref, g_ref, o_ref, *, tq):
    i = pl.program_id(0)
    t_row = i * tq + (_iota((NSA_REP * tq, 1), 0) & (tq - 1))
    kp = jnp.concatenate([(i - 2 + j) * tq + _iota((1, tq), 1) for j in range(3)], axis=1)
    mask = (kp >= 0) & (kp <= t_row) & (kp > t_row - WINDOW)
    gates = jax.nn.sigmoid(sm_ref[...])
    outs = []
    for g in range(NSA_GROUPS):
        gs = slice(g * HEAD_DIM, (g + 1) * HEAD_DIM)
        qs = _stack_heads(q_ref[:, g * NSA_REP * HEAD_DIM:(g + 1) * NSA_REP * HEAD_DIM])
        k = jnp.concatenate([k0_ref[:, gs], k1_ref[:, gs], k2_ref[:, gs]], axis=0)
        v = jnp.concatenate([v0_ref[:, gs], v1_ref[:, gs], v2_ref[:, gs]], axis=0)
        p = _masked_softmax2(_dot_nt(qs, k), mask)
        ow = _dot(p.astype(BF16), v)
        for h in range(NSA_REP):
            hh = g * NSA_REP + h
            hs = slice(hh * HEAD_DIM, (hh + 1) * HEAD_DIM)
            c = SM_GATE + 3 * hh
            o = (gates[:, c:c + 1] * oc_ref[:, hs] + gates[:, c + 1:c + 2] * os_ref[:, hs]
                 + gates[:, c + 2:c + 3] * ow[h * tq:(h + 1) * tq])
            o = o * lax.rsqrt(jnp.mean(o * o, -1, keepdims=True) + 1e-6) * g_ref[:, hs]
            outs.append(o)
    o_ref[...] = jnp.concatenate(outs, axis=1).astype(o_ref.dtype)


def _win_combine(pb, pf, o_cmp, o_sel, out_g):
    t = pb.shape[0]
    tq = 256
    wq = NSA_HEADS * HEAD_DIM
    wkv = NSA_GROUPS * HEAD_DIM

    def kvspec(j, unit):
        return pl.BlockSpec((tq, wkv), lambda i: (jnp.maximum(i - 2 + j, 0), unit // NSA_GROUPS))

    full = pl.BlockSpec((tq, wq), lambda i: (i, 0))
    return pl.pallas_call(
        functools.partial(_win_body, tq=tq),
        grid=(t // tq,),
        in_specs=[full,
                  kvspec(0, PB_KW), kvspec(1, PB_KW), kvspec(2, PB_KW),
                  kvspec(0, PB_VW), kvspec(1, PB_VW), kvspec(2, PB_VW),
                  full, full,
                  pl.BlockSpec((tq, LANES), lambda i: (i, PF_SMALL)),
                  pl.BlockSpec((1, wq), lambda i: (0, 0))],
        out_specs=full,
        out_shape=jax.ShapeDtypeStruct((t, wq), BF16),
        compiler_params=_cparams(("arbitrary",)),
        name="nsa_win_combine",
    )(pb, pb, pb, pb, pb, pb, pb, o_cmp, o_sel, pf, out_g)


def _gprep_body(x_ref, halo_ref, cw_ref, sm_ref, al_ref, dt_ref, qn_ref, kn_ref, v_ref, g_ref, b_ref, *, tm):
    i = pl.program_id(0)
    x = x_ref[...]
    halo = halo_ref[...] * (i > 0).astype(F32)
    xe = jnp.concatenate([halo, x], axis=0)
    cw = cw_ref[...]
    y = cw[GDN_CONV - 1:GDN_CONV] * x
    for j in range(GDN_CONV - 1):
        off = 8 - (GDN_CONV - 1) + j
        y = y + cw[j:j + 1] * xe[off:off + tm]
    y = y * jax.nn.sigmoid(y)
    nqk = GDN_QK_HEADS * HEAD_DIM
    for h in range(GDN_QK_HEADS):
        hs = slice(h * HEAD_DIM, (h + 1) * HEAD_DIM)
        q = y[:, hs]
        k = y[:, nqk + h * HEAD_DIM:nqk + (h + 1) * HEAD_DIM]
        qn_ref[:, hs] = q * lax.rsqrt(jnp.sum(q * q, -1, keepdims=True) + 1e-6)
        kn_ref[:, hs] = k * lax.rsqrt(jnp.sum(k * k, -1, keepdims=True) + 1e-6)
    v_ref[...] = y[:, 2 * nqk:]
    sm = sm_ref[...]
    b_ref[...] = jax.nn.sigmoid(sm)
    z = sm + dt_ref[...]
    softplus = jnp.maximum(z, 0.0) + jnp.log(1.0 + jnp.exp(-jnp.abs(z)))
    g_ref[...] = -jnp.exp(al_ref[...]) * softplus


def _gdn_prep(pf, conv_w, alog_pad, dt_pad):
    t = pf.shape[0]
    tm = 512
    cch = 2 * GDN_QK_HEADS * HEAD_DIM + GDN_V_HEADS * HEAD_DIM
    nqk = GDN_QK_HEADS * HEAD_DIM
    nv = GDN_V_HEADS * HEAD_DIM
    u = PF_QG * LANES // cch
    lane = pl.BlockSpec((1, LANES), lambda i: (0, 0))
    row = lambda w: pl.BlockSpec((tm, w), lambda i: (i, 0))
    return pl.pallas_call(
        functools.partial(_gprep_body, tm=tm),
        grid=(t // tm,),
        in_specs=[pl.BlockSpec((tm, cch), lambda i: (i, u)),
                  pl.BlockSpec((8, cch), lambda i: (jnp.maximum(i * (tm // 8) - 1, 0), u)),
                  pl.BlockSpec((GDN_CONV, cch), lambda i: (0, 0)),
                  pl.BlockSpec((tm, LANES), lambda i: (i, PF_SMALL)),
                  lane, lane],
        out_specs=[row(nqk), row(nqk), row(nv), row(LANES), row(LANES)],
        out_shape=[jax.ShapeDtypeStruct((t, nqk), F32), jax.ShapeDtypeStruct((t, nqk), F32),
                   jax.ShapeDtypeStruct((t, nv), F32), jax.ShapeDtypeStruct((t, LANES), F32),
                   jax.ShapeDtypeStruct((t, LANES), F32)],
        compiler_params=_cparams(("arbitrary",)),
        name="gdn_prep",
    )(pf, pf, conv_w, pf, alog_pad, dt_pad)


def _unit_lower_inverse(a, blk16):
    c = a.shape[0]
    eye = (_iota((c, c), 0) == _iota((c, c), 1)).astype(F32)
    ad = jnp.where(blk16, a, 0.0)
    ao = a - ad
    x2 = _dot3(ad, ad)
    x4 = _dot3(x2, x2)
    x8 = _dot3(x4, x4)
    p = eye - ad
    p = p + _dot3(p, x2)
    p = p + _dot3(p, x4)
    p = p + _dot3(p, x8)
    m = _dot3(p, ao)
    m2 = _dot3(m, m)
    t1 = p + _dot3(m2, p)
    return t1 - _dot3(m, t1)


def _gscan_body(q_ref, k_ref, v_ref, g_ref, b_ref, z_ref, ng_ref, o_ref, s_ref, *, chunks):
    c = GDN_CHUNK

    @pl.when(pl.program_id(0) == 0)
    def _():
        s_ref[...] = jnp.zeros(s_ref.shape, F32)

    r_i = _iota((c, c), 0)
    c_i = _iota((c, c), 1)
    tri = r_i >= c_i
    strict = r_i > c_i
    blk16 = (r_i >> 4) == (c_i >> 4)
    lower = tri.astype(BF16)
    upper = (r_i <= c_i).astype(BF16)
    ng = ng_ref[...]
    rep = GDN_V_HEADS // GDN_QK_HEADS

    def chunk(n, carry):
        rows = pl.ds(pl.multiple_of(n * c, c), c)
        gf = g_ref[rows, :]
        bf = b_ref[rows, :]
        gc_all = _dot_ones_l(lower, gf)
        gcr_all = _dot_ones_r(gf.T, upper)
        for h in range(GDN_V_HEADS):
            hq = h // rep
            qs_ = slice(hq * HEAD_DIM, (hq + 1) * HEAD_DIM)
            vs_ = slice(h * HEAD_DIM, (h + 1) * HEAD_DIM)
            gc = gc_all[:, SM_A + h:SM_A + h + 1]
            gcr = gcr_all[SM_A + h:SM_A + h + 1, :]
            beta = bf[:, SM_BETA + h:SM_BETA + h + 1]
            decay = jnp.where(tri, jnp.exp(jnp.where(tri, gc - gcr, 0.0)), 0.0)
            kh = k_ref[rows, qs_]
            qh = q_ref[rows, qs_] * (HEAD_DIM ** -0.5)
            kb = kh * beta
            khb = kh.astype(BF16)
            a = jnp.where(strict, _dot_nt(kb.astype(BF16), khb) * decay, 0.0)
            qk = jnp.where(tri, _dot_nt(qh.astype(BF16), khb) * decay, 0.0)
            egc = jnp.exp(gc)
            tinv = _unit_lower_inverse(a, blk16)
            uw = _dot3(tinv, jnp.concatenate([v_ref[rows, vs_] * beta, kb * egc], axis=1))
            u = uw[:, :HEAD_DIM]
            w = uw[:, HEAD_DIM:]
            g_last = gc[c - 1:c]
            k_dec = kh * jnp.exp(g_last - gc)
            st = s_ref[h]
            r = _dot(jnp.concatenate([w, qh * egc], axis=0).astype(BF16), st.astype(BF16))
            v_new = u - r[:c]
            o = r[c:] + _dot(qk.astype(BF16), v_new.astype(BF16))
            s_ref[h] = st * jnp.exp(g_last) + _dot(k_dec.T.astype(BF16), v_new.astype(BF16))
            zh = z_ref[rows, vs_]
            o = o * lax.rsqrt(jnp.mean(o * o, -1, keepdims=True) + 1e-6) * ng * (zh * jax.nn.sigmoid(zh))
            o_ref[rows, vs_] = o.astype(o_ref.dtype)
        return carry

    lax.fori_loop(0, chunks, chunk, 0)


def _gdn_scan(qn, kn, v, gfull, bfull, pf, norm_g):
    t = qn.shape[0]
    chunks = 4
    tm = chunks * GDN_CHUNK
    nqk = GDN_QK_HEADS * HEAD_DIM
    nv = GDN_V_HEADS * HEAD_DIM
    row = lambda w: pl.BlockSpec((tm, w), lambda i: (i, 0))
    return pl.pallas_call(
        functools.partial(_gscan_body, chunks=chunks),
        grid=(t // tm,),
        in_specs=[row(nqk), row(nqk), row(nv), row(LANES), row(LANES),
                  pl.BlockSpec((tm, nv), lambda i: (i, PF_Z * LANES // nv)),
                  pl.BlockSpec((1, HEAD_DIM), lambda i: (0, 0))],
        out_specs=row(nv),
        out_shape=jax.ShapeDtypeStruct((t, nv), BF16),
        scratch_shapes=[pltpu.VMEM((GDN_V_HEADS, HEAD_DIM, HEAD_DIM), F32)],
        compiler_params=_cparams(("arbitrary",)),
        name="gdn_scan",
    )(qn, kn, v, gfull, bfull, pf, norm_g)


def _post_body(on_ref, og_ref, w_ref, x_ref, gm_ref, lg_ref, lb_ref, sh_ref, sc_ref, rw_ref, rb_ref, tril_ref,
               x1_ref, hp_ref, ti_ref, gt_ref, rk_ref, cnt_ref, carry_ref):
    half = D_MODEL // 2

    @pl.when(pl.program_id(0) == 0)
    def _():
        carry_ref[...] = jnp.zeros(carry_ref.shape, F32)

    mix = _dot(on_ref[...], w_ref[0:half, :]) + _dot(og_ref[...], w_ref[half:, :])
    x1 = _ln_plain(DEEPNORM_ALPHA * x_ref[...] + gm_ref[...] * mix, 1e-5) * lg_ref[...] + lb_ref[...]
    x1_ref[...] = x1
    h2 = _ln_plain(x1, 1e-6) * (1.0 + sc_ref[...]) + sh_ref[...]
    lo = lax.bitcast_convert_type(h2[:, :half].astype(BF16).astype(F32), U32) >> 16
    hi = lax.bitcast_convert_type(h2[:, half:].astype(BF16).astype(F32), U32) & jnp.uint32(0xFFFF0000)
    hp_ref[...] = lo | hi
    sc = _dot3(h2, rw_ref[...]) + rb_ref[...]
    lane = _iota((1, LANES), 1).astype(F32)
    hot = jnp.zeros(sc.shape, F32)
    vals, idxs = [], []
    for _ in range(TOP_K):
        m = jnp.max(sc, -1, keepdims=True)
        idx = jnp.min(jnp.where(sc == m, lane, float(LANES)), -1, keepdims=True)
        pick = lane == idx
        hot = jnp.where(pick, 1.0, hot)
        sc = jnp.where(pick, -jnp.inf, sc)
        vals.append(m)
        idxs.append(idx)
    ex = [jnp.exp(v - vals[0]) for v in vals]
    den = (ex[0] + ex[1]) + (ex[2] + ex[3])
    before = carry_ref[...] + _dot(tril_ref[...], hot.astype(BF16))
    ti = jnp.zeros(sc.shape, F32)
    gt = jnp.zeros(sc.shape, F32)
    rk = jnp.zeros(sc.shape, F32)
    for k in range(TOP_K):
        at_k = lane == float(k)
        ti = jnp.where(at_k, idxs[k], ti)
        gt = jnp.where(at_k, ex[k] / den, gt)
        rk = jnp.where(at_k, jnp.sum(jnp.where(lane == idxs[k], before, 0.0), -1, keepdims=True), rk)
    ti_ref[...] = ti.astype(I32)
    gt_ref[...] = gt
    rk_ref[...] = rk.astype(I32)
    carry_ref[...] = carry_ref[...] + jnp.sum(hot, axis=0, keepdims=True)
    cnt_ref[...] = carry_ref[...]


def _post_mixer(o_nsa, o_gdn, w_out, x, gm1, ln_g, ln_b, sh_f, sc_f, rw_pad, rb_pad, tril):
    t, d = x.shape
    tm = tril.shape[0]
    half = d // 2
    vec = pl.BlockSpec((1, d), lambda i: (0, 0))
    lane_vec = pl.BlockSpec((1, LANES), lambda i: (0, 0))
    row = lambda w: pl.BlockSpec((tm, w), lambda i: (i, 0))
    return pl.pallas_call(
        _post_body,
        grid=(t // tm,),
        in_specs=[row(half), row(half), pl.BlockSpec((d, d), lambda i: (0, 0)), row(d),
                  vec, vec, vec, vec, vec,
                  pl.BlockSpec((d, LANES), lambda i: (0, 0)), lane_vec,
                  pl.BlockSpec((tm, tm), lambda i: (0, 0))],
        out_specs=[row(d), row(half), row(LANES), row(LANES), row(LANES), lane_vec],
        out_shape=[jax.ShapeDtypeStruct((t, d), F32), jax.ShapeDtypeStruct((t, half), U32),
                   jax.ShapeDtypeStruct((t, LANES), I32), jax.ShapeDtypeStruct((t, LANES), F32),
                   jax.ShapeDtypeStruct((t, LANES), I32), jax.ShapeDtypeStruct((1, LANES), F32)],
        scratch_shapes=[pltpu.VMEM((1, LANES), F32)],
        compiler_params=_cparams(("arbitrary",)),
        name="post_mixer_router",
    )(o_nsa, o_gdn, w_out, x, gm1, ln_g, ln_b, sh_f, sc_f, rw_pad, rb_pad, tril)


def _row_copy(src_ref, dst_ref, sem):
    return pltpu.make_async_copy(src_ref, dst_ref, sem)


def _dispatch_body(dest_ref, hp_ref, xs_in_ref, xs_ref, sem, *, tm):
    del xs_in_ref

    def issue(r, carry):
        for k in range(TOP_K):
            _row_copy(hp_ref.at[pl.ds(r, 1)], xs_ref.at[pl.ds(dest_ref[r * TOP_K + k], 1)], sem).start()
        return carry

    def drain(r, carry):
        for k in range(TOP_K):
            _row_copy(hp_ref.at[pl.ds(0, 1)], xs_ref.at[pl.ds(0, 1)], sem).wait()
        return carry

    lax.fori_loop(0, tm, issue, 0)
    lax.fori_loop(0, tm, drain, 0)


def _dispatch(dest, hp, rows):
    t, w = hp.shape
    tm = 256
    zeros = jnp.zeros((rows, w), hp.dtype)
    return pl.pallas_call(
        functools.partial(_dispatch_body, tm=tm),
        grid=(t // tm,),
        in_specs=[pl.BlockSpec((tm * TOP_K,), lambda i: (i,), memory_space=pltpu.SMEM),
                  pl.BlockSpec((tm, w), lambda i: (i, 0)),
                  pl.BlockSpec(memory_space=pl.ANY)],
        out_specs=pl.BlockSpec(memory_space=pl.ANY),
        out_shape=jax.ShapeDtypeStruct((rows, w), hp.dtype),
        scratch_shapes=[pltpu.SemaphoreType.DMA(())],
        input_output_aliases={2: 0},
        compiler_params=_cparams(("arbitrary",)),
        name="moe_dispatch",
    )(dest, hp, zeros)


def _deinterleave_body(w_ref, p_ref, g_ref, l_ref):
    perm = p_ref[...]
    for c in range(w_ref.shape[2] // (2 * LANES)):
        r = _dot(w_ref[0, :, c * 2 * LANES:(c + 1) * 2 * LANES].astype(BF16), perm)
        g_ref[0, :, c * LANES:(c + 1) * LANES] = r[:, :LANES].astype(g_ref.dtype)
        l_ref[0, :, c * LANES:(c + 1) * LANES] = r[:, LANES:].astype(l_ref.dtype)


def _split_gate_up(w_gu):
    e, d, n = w_gu.shape
    tr = 512
    perm = np.zeros((2 * LANES, 2 * LANES), np.float32)
    perm[np.arange(0, 2 * LANES, 2), np.arange(LANES)] = 1.0
    perm[np.arange(1, 2 * LANES, 2), LANES + np.arange(LANES)] = 1.0
    out = jax.ShapeDtypeStruct((e, d, n // 2), BF16)
    ospec = pl.BlockSpec((1, tr, n // 2), lambda i, j: (i, j, 0))
    return pl.pallas_call(
        _deinterleave_body,
        grid=(e, d // tr),
        in_specs=[pl.BlockSpec((1, tr, n), lambda i, j: (i, j, 0)),
                  pl.BlockSpec((2 * LANES, 2 * LANES), lambda i, j: (0, 0))],
        out_specs=[ospec, ospec],
        out_shape=[out, out],
        compiler_params=_cparams(("arbitrary", "arbitrary")),
        name="moe_split_gate_up",
    )(w_gu, jnp.asarray(perm, dtype=BF16))


def _gu_body(be_ref, xs_ref, wg_ref, wl_ref, bg_ref, bl_ref, o_ref):
    del be_ref
    half = D_MODEL // 2
    u = xs_ref[...]
    xa = lax.bitcast_convert_type(u << 16, F32).astype(BF16)
    xb = lax.bitcast_convert_type(u & jnp.uint32(0xFFFF0000), F32).astype(BF16)
    tn = 512
    for c in range(D_FF // tn):
        cs = slice(c * tn, (c + 1) * tn)
        hg = _dot(xa, wg_ref[0, 0:half, cs]) + _dot(xb, wg_ref[0, half:, cs]) + bg_ref[0, :, cs]
        hl = _dot(xa, wl_ref[0, 0:half, cs]) + _dot(xb, wl_ref[0, half:, cs]) + bl_ref[0, :, cs]
        x_glu = jnp.minimum(hg, SWIGLU_LIMIT)
        x_lin = jnp.clip(hl, -SWIGLU_LIMIT, SWIGLU_LIMIT)
        o_ref[:, cs] = (x_glu * jax.nn.sigmoid(SWIGLU_ALPHA * x_glu) * (x_lin + 1.0)).astype(o_ref.dtype)


def _expert_gu(blk_exp, xs, wg, wl, bg, bl):
    rows, half = xs.shape
    nb = rows // EXPERT_ROWS
    wspec = pl.BlockSpec((1, D_MODEL, D_FF), lambda b, be: (be[b], 0, 0))
    bspec = pl.BlockSpec((1, 1, D_FF), lambda b, be: (be[b], 0, 0))
    grid_spec = pltpu.PrefetchScalarGridSpec(
        num_scalar_prefetch=1,
        grid=(nb,),
        in_specs=[pl.BlockSpec((EXPERT_ROWS, half), lambda b, be: (b, 0)), wspec, wspec, bspec, bspec],
        out_specs=pl.BlockSpec((EXPERT_ROWS, D_FF), lambda b, be: (b, 0)),
    )
    return pl.pallas_call(
        _gu_body,
        grid_spec=grid_spec,
        out_shape=jax.ShapeDtypeStruct((rows, D_FF), BF16),
        compiler_params=_cparams(("arbitrary",)),
        name="moe_gate_up",
    )(blk_exp, xs, wg, wl, bg, bl)


def _dn_body(be_ref, h_ref, w_ref, b_ref, o_ref):
    del be_ref
    o_ref[...] = _dot(h_ref[...], w_ref[0]) + b_ref[0]


def _expert_dn(blk_exp, hact, wd, bd):
    rows = hact.shape[0]
    nb = rows // EXPERT_ROWS
    grid_spec = pltpu.PrefetchScalarGridSpec(
        num_scalar_prefetch=1,
        grid=(nb,),
        in_specs=[pl.BlockSpec((EXPERT_ROWS, D_FF), lambda b, be: (b, 0)),
                  pl.BlockSpec((1, D_FF, D_MODEL), lambda b, be: (be[b], 0, 0)),
                  pl.BlockSpec((1, 1, D_MODEL), lambda b, be: (be[b], 0, 0))],
        out_specs=pl.BlockSpec((EXPERT_ROWS, D_MODEL), lambda b, be: (b, 0)),
    )
    return pl.pallas_call(
        _dn_body,
        grid_spec=grid_spec,
        out_shape=jax.ShapeDtypeStruct((rows, D_MODEL), F32),
        compiler_params=_cparams(("arbitrary",)),
        name="moe_down",
    )(blk_exp, hact, wd, bd)


def _combine_body(dest_ref, y_ref, gt_ref, x1_ref, gf_ref, lg_ref, lb_ref, o_ref, buf_ref, sem, *, tm):
    def issue(r, carry):
        for k in range(TOP_K):
            _row_copy(y_ref.at[pl.ds(dest_ref[r * TOP_K + k], 1)], buf_ref.at[k, pl.ds(r, 1)], sem).start()
        return carry

    def drain(r, carry):
        for k in range(TOP_K):
            _row_copy(y_ref.at[pl.ds(0, 1)], buf_ref.at[k, pl.ds(0, 1)], sem).wait()
        return carry

    lax.fori_loop(0, tm, issue, 0)
    lax.fori_loop(0, tm, drain, 0)
    gt = gt_ref[...]
    y = (gt[:, 0:1] * buf_ref[0] + gt[:, 1:2] * buf_ref[1]) + (gt[:, 2:3] * buf_ref[2] + gt[:, 3:4] * buf_ref[3])
    o_ref[...] = _ln_plain(DEEPNORM_ALPHA * x1_ref[...] + gf_ref[...] * y, 1e-5) * lg_ref[...] + lb_ref[...]


def _combine(dest, y, gt, x1, gf1, ln_g, ln_b):
    t, d = x1.shape
    tm = 256
    vec = pl.BlockSpec((1, d), lambda i: (0, 0))
    return pl.pallas_call(
        functools.partial(_combine_body, tm=tm),
        grid=(t // tm,),
        in_specs=[pl.BlockSpec((tm * TOP_K,), lambda i: (i,), memory_space=pltpu.SMEM),
                  pl.BlockSpec(memory_space=pl.ANY),
                  pl.BlockSpec((tm, LANES), lambda i: (i, 0)),
                  pl.BlockSpec((tm, d), lambda i: (i, 0)), vec, vec, vec],
        out_specs=pl.BlockSpec((tm, d), lambda i: (i, 0)),
        out_shape=jax.ShapeDtypeStruct((t, d), F32),
        scratch_shapes=[pltpu.VMEM((TOP_K, tm, d), F32), pltpu.SemaphoreType.DMA(())],
        compiler_params=_cparams(("arbitrary",)),
        name="moe_combine",
    )(dest, y, gt, x1, gf1, ln_g, ln_b)


def _overlap_matrix(t):
    nc_pad = t // CMP_STRIDE
    n_c = (t - CMP_BLOCK) // CMP_STRIDE + 1
    n_s = t // SEL_BLOCK
    nsp = -(-n_s // LANES) * LANES
    ci = np.arange(nc_pad)[:, None] * CMP_STRIDE
    sj = np.arange(nsp)[None, :] * SEL_BLOCK
    ov = (ci < sj + SEL_BLOCK) & (ci + CMP_BLOCK > sj)
    ov &= (np.arange(nc_pad)[:, None] < n_c) & (np.arange(nsp)[None, :] < n_s)
    return jnp.asarray(ov.astype(np.float32), dtype=BF16)


def _block_onehot():
    e = (np.arange(SEL_SUPER)[:, None] // SEL_BLOCK) == np.arange(LANES)[None, :]
    return jnp.asarray(e.astype(np.float32), dtype=BF16)


def _in_proj_weights(w_in):
    nq = NSA_HEADS * HEAD_DIM
    nkv = NSA_GROUPS * HEAD_DIM
    o = np.cumsum([0, nq, nkv, nkv, nkv, nkv, nkv, nkv, 3 * NSA_HEADS,
                   GDN_QK_HEADS * HEAD_DIM, GDN_QK_HEADS * HEAD_DIM, GDN_V_HEADS * HEAD_DIM,
                   GDN_V_HEADS * HEAD_DIM, GDN_V_HEADS, GDN_V_HEADS]).tolist()
    q, kc, vc, ks, vs, kw, vw, gate, qg, kg, vg, z, b_l, a_l = [w_in[:, o[i]:o[i + 1]] for i in range(14)]
    wb = jnp.concatenate([q, ks, vs, kw, vw], axis=1).astype(BF16)
    small = jnp.concatenate([gate, b_l, a_l], axis=1)
    pad = jnp.zeros((w_in.shape[0], 4 * LANES - small.shape[1]), w_in.dtype)
    wf = jnp.concatenate([qg, kg, vg, z, kc, vc, small, pad], axis=1).astype(BF16)
    cs_b = jnp.concatenate([jnp.full((1, nq), HEAD_DIM ** -0.5 * LOG2E, F32), jnp.ones((1, wb.shape[1] - nq), F32)],
                           axis=1)
    cs_f = jnp.ones((1, wf.shape[1]), F32)
    return wb, cs_b, wf, cs_f


def _lane_pad(v, start):
    return jnp.zeros((1, LANES), F32).at[0, start:start + v.shape[0]].set(v)


def _layer(x, mod, w_in, cmp_pos, cmp_w1, cmp_w2, nsa_out_g, conv_w, a_log, dt_bias, gdn_norm_g, w_out,
           ln_mix_g, ln_mix_b, router_w, router_b, w_gu, b_gu, w_dn, b_dn, ln_ffn_g, ln_ffn_b, consts):
    t, d = x.shape
    ov, etab, tril = consts
    sh_m, sc_m, g_m, sh_f, sc_f, g_f = [mod[:, i * d:(i + 1) * d] for i in range(6)]

    h = _lnmod(x, sh_m, sc_m)
    wb, cs_b, wf, cs_f = _in_proj_weights(w_in)
    pb = _matmul(h, wb, cs_b, BF16)
    pf = _matmul(h, wf, cs_f, F32)
    nc = t // CMP_STRIDE
    x4 = jnp.stack([pf[:, u * LANES:(u + 1) * LANES].reshape(nc, CMP_STRIDE * HEAD_DIM)
                    for u in range(PF_KC, PF_KC + 2 * NSA_GROUPS)])
    cc = _compress(x4, cmp_pos, cmp_w1, cmp_w2)
    n_sel = min(SEL_COUNT, t // SEL_BLOCK)
    o_cmp, mb = _cmp_select(pb, cc, ov, n_sel)
    o_sel = _sel_attention(pb, mb, etab)
    o_nsa = _win_combine(pb, pf, o_cmp, o_sel, nsa_out_g.reshape(1, -1))
    qn, kn, v, gfull, bfull = _gdn_prep(pf, conv_w.reshape(GDN_CONV, -1), _lane_pad(a_log, SM_A),
                                        _lane_pad(dt_bias, SM_A))
    o_gdn = _gdn_scan(qn, kn, v, gfull, bfull, pf, gdn_norm_g.reshape(1, -1))

    rw_pad = jnp.zeros((d, LANES), F32).at[:, :N_EXPERTS].set(router_w)
    rb_pad = jnp.full((1, LANES), -jnp.inf, F32).at[0, :N_EXPERTS].set(router_b)
    x1, hp, ti, gt, rk, cnt = _post_mixer(o_nsa, o_gdn, w_out.astype(BF16), x, 1.0 + g_m, ln_mix_g.reshape(1, d),
                                           ln_mix_b.reshape(1, d), sh_f, sc_f, rw_pad, rb_pad, tril)

    counts = cnt[0, :N_EXPERTS].astype(I32)
    padded = (counts + EXPERT_ROWS - 1) // EXPERT_ROWS * EXPERT_ROWS
    pad_end = jnp.cumsum(padded)
    pad_start = pad_end - padded
    dest = (pad_start[ti[:, :TOP_K]] + rk[:, :TOP_K]).reshape(-1)
    nb = t * TOP_K // EXPERT_ROWS + N_EXPERTS
    blk_exp = jnp.minimum(jnp.searchsorted(pad_end, jnp.arange(nb, dtype=I32) * EXPERT_ROWS, side="right"),
                          N_EXPERTS - 1).astype(I32)

    xs = _dispatch(dest, hp, nb * EXPERT_ROWS)
    wg, wl = _split_gate_up(w_gu)
    bg = b_gu[:, None, 0::2]
    bl = b_gu[:, None, 1::2]
    hact = _expert_gu(blk_exp, xs, wg, wl, bg, bl)
    y = _expert_dn(blk_exp, hact, w_dn.astype(BF16), b_dn[:, None, :])
    return _combine(dest, y, gt, x1, 1.0 + g_f, ln_ffn_g.reshape(1, d), ln_ffn_b.reshape(1, d))


def kernel(x, c, w_in, cmp_pos_k, cmp_pos_v, cmp_k_w1, cmp_k_w2, cmp_v_w1, cmp_v_w2, nsa_out_g, gdn_conv_w,
           gdn_a_log, gdn_dt_bias, gdn_norm_g, w_out, w_ada, b_ada, ln_mix_g, ln_mix_b, router_w, router_b,
           w_gu, b_gu, w_dn, b_dn, ln_ffn_g, ln_ffn_b):
    bsz, t, d = x.shape
    assert bsz == 1 and d == D_MODEL and t % 1024 == 0
    nl = w_in.shape[0]
    consts = (_overlap_matrix(t), _block_onehot(),
              jnp.asarray(np.tril(np.ones((512, 512), np.float32), -1), dtype=BF16))
    mod = _ada(c, w_ada, b_ada)
    half_w = CMP_STRIDE * HEAD_DIM
    xt = x.reshape(t, d)
    for l in range(nl):
        pos = jnp.stack([cmp_pos_k[l], cmp_pos_v[l]]).reshape(2, 2, 1, half_w)
        w1 = jnp.stack([cmp_k_w1[l], cmp_v_w1[l]]).reshape(2, 2, half_w, CMP_HIDDEN).astype(BF16)
        w2 = jnp.stack([cmp_k_w2[l], cmp_v_w2[l]]).astype(BF16)
        xt = _layer(xt, mod[l], w_in[l], pos, w1, w2, nsa_out_g[l], gdn_conv_w[l], gdn_a_log[l], gdn_dt_bias[l],
                    gdn_norm_g[l], w_out[l], ln_mix_g[l], ln_mix_b[l], router_w[l], router_b[l], w_gu[l], b_gu[l],
                    w_dn[l], b_dn[l], ln_ffn_g[l], ln_ffn_b[l], consts)
    return xt.reshape(bsz, t, d)
```

```python
import functools

import numpy as np
import jax
import jax.numpy as jnp
from jax import lax
from jax.experimental import pallas as pl
from jax.experimental.pallas import tpu as pltpu

F32 = jnp.float32
BF16 = jnp.bfloat16
I32 = jnp.int32
U32 = jnp.uint32

D_MODEL = 2048
DEPTH = 2
HEAD_DIM = 128
NSA_HEADS = 8
NSA_GROUPS = 2
NSA_REP = NSA_HEADS // NSA_GROUPS
CMP_BLOCK = 32
CMP_STRIDE = 16
CMP_HIDDEN = 256
SEL_BLOCK = 64
SEL_COUNT = 16
WINDOW = 512
GDN_QK_HEADS = 4
GDN_V_HEADS = 8
GDN_CONV = 4
GDN_CHUNK = 64
N_EXPERTS = 32
TOP_K = 4
D_FF = 2048
SWIGLU_LIMIT = 7.0
SWIGLU_ALPHA = 1.702
DEEPNORM_ALPHA = (2 * DEPTH) ** 0.25

LANES = 128
VMEM_LIMIT = 56 * 1024 * 1024
NEG = -1e30
LOG2E = 1.4426950408889634
SEL_SUPER = LANES * SEL_BLOCK
EXPERT_ROWS = 512

PB_Q, PB_KS, PB_VS, PB_KW, PB_VW = 0, 8, 10, 12, 14
PF_QG, PF_KG, PF_VG, PF_Z, PF_KC, PF_VC, PF_SMALL = 0, 4, 8, 16, 24, 26, 28
SM_GATE, SM_BETA, SM_A = 0, 24, 32


def _cparams(sem):
    return pltpu.CompilerParams(dimension_semantics=sem, vmem_limit_bytes=VMEM_LIMIT)


def _dot(a, b):
    return jnp.dot(a, b, preferred_element_type=F32)


def _dot_nt(a, b):
    return lax.dot_general(a, b, (((1,), (1,)), ((), ())), preferred_element_type=F32)


def _split2(a):
    hi = a.astype(BF16)
    return hi, (a - hi.astype(F32)).astype(BF16)


def _split3(a):
    hi = a.astype(BF16)
    r = a - hi.astype(F32)
    mid = r.astype(BF16)
    return hi, mid, (r - mid.astype(F32)).astype(BF16)


def _dot3(a, b):
    ah, al = _split2(a)
    bh, bl = _split2(b)
    return _dot(ah, bh) + (_dot(ah, bl) + _dot(al, bh))


def _dot_ones_l(ones_bf16, x):
    h, m, l = _split3(x)
    return _dot(ones_bf16, h) + (_dot(ones_bf16, m) + _dot(ones_bf16, l))


def _dot_ones_r(x, ones_bf16):
    h, m, l = _split3(x)
    return _dot(h, ones_bf16) + (_dot(m, ones_bf16) + _dot(l, ones_bf16))


def _ln_plain(x, eps):
    mu = jnp.mean(x, -1, keepdims=True)
    xc = x - mu
    var = jnp.mean(xc * xc, -1, keepdims=True)
    return xc * lax.rsqrt(var + eps)


def _masked_softmax2(s2, mask):
    s2 = jnp.where(mask, s2, -jnp.inf)
    m = jnp.max(s2, -1, keepdims=True)
    m = jnp.where(jnp.abs(m) < jnp.inf, m, 0.0)
    p = jnp.exp2(s2 - m)
    return p / jnp.maximum(jnp.sum(p, -1, keepdims=True), 1e-30)


def _iota(shape, dim):
    return lax.broadcasted_iota(I32, shape, dim)


def _stack_heads(q):
    return jnp.concatenate([q[:, h * HEAD_DIM:(h + 1) * HEAD_DIM] for h in range(NSA_REP)], axis=0)


def _unstack_heads(o, t):
    return jnp.concatenate([o[h * t:(h + 1) * t] for h in range(NSA_REP)], axis=1)


def _ada_body(c_ref, w_ref, b_ref, o_ref):
    c = c_ref[...]
    ca = c * jax.nn.sigmoid(c)
    o_ref[0] = jnp.sum(w_ref[0] * ca, axis=0, keepdims=True) + b_ref[0]


def _ada(c, w_ada, b_ada):
    nl, d, n = w_ada.shape
    tn = 1024
    return pl.pallas_call(
        _ada_body,
        grid=(nl, n // tn),
        in_specs=[pl.BlockSpec((d, 1), lambda l, j: (0, 0)),
                  pl.BlockSpec((1, d, tn), lambda l, j: (l, 0, j)),
                  pl.BlockSpec((1, 1, tn), lambda l, j: (l, 0, j))],
        out_specs=pl.BlockSpec((1, 1, tn), lambda l, j: (l, 0, j)),
        out_shape=jax.ShapeDtypeStruct((nl, 1, n), F32),
        compiler_params=_cparams(("arbitrary", "arbitrary")),
        name="ada_mod",
    )(c.reshape(d, 1), w_ada, b_ada.reshape(nl, 1, n))


def _lnmod_body(x_ref, sh_ref, sc_ref, o_ref):
    h = _ln_plain(x_ref[...], 1e-6) * (1.0 + sc_ref[...]) + sh_ref[...]
    o_ref[...] = h.astype(o_ref.dtype)


def _lnmod(x, shift, scale):
    t, d = x.shape
    tm = 512
    vec = pl.BlockSpec((1, d), lambda i: (0, 0))
    return pl.pallas_call(
        _lnmod_body,
        grid=(t // tm,),
        in_specs=[pl.BlockSpec((tm, d), lambda i: (i, 0)), vec, vec],
        out_specs=pl.BlockSpec((tm, d), lambda i: (i, 0)),
        out_shape=jax.ShapeDtypeStruct((t, d), BF16),
        compiler_params=_cparams(("arbitrary",)),
        name="ln_mod",
    )(x, shift, scale)


def _mm_body(a_ref, w_ref, cs_ref, o_ref):
    o_ref[...] = (_dot(a_ref[...], w_ref[...]) * cs_ref[...]).astype(o_ref.dtype)


def _matmul(a, w, colscale, out_dtype):
    m, k = a.shape
    n = w.shape[1]
    tm, tn = 1024, 512
    return pl.pallas_call(
        _mm_body,
        grid=(m // tm, n // tn),
        in_specs=[pl.BlockSpec((tm, k), lambda i, j: (i, 0)),
                  pl.BlockSpec((k, tn), lambda i, j: (0, j)),
                  pl.BlockSpec((1, tn), lambda i, j: (0, j))],
        out_specs=pl.BlockSpec((tm, tn), lambda i, j: (i, j)),
        out_shape=jax.ShapeDtypeStruct((m, n), out_dtype),
        compiler_params=_cparams(("arbitrary", "arbitrary")),
        name="proj_mm",
    )(a, w, colscale)


def _compress_body(x_ref, p_ref, w1_ref, w2_ref, o_ref):
    nc = o_ref.shape[1]
    y = jnp.zeros((nc, CMP_HIDDEN), F32)
    z = jnp.zeros((nc, CMP_HIDDEN), F32)
    for r in range(CMP_STRIDE):
        xr = x_ref[pl.ds(r, nc, stride=CMP_STRIDE), :]
        y = y + _dot((xr + p_ref[0, r:r + 1, :]).astype(BF16), w1_ref[0, r])
        z = z + _dot((xr + p_ref[0, CMP_STRIDE + r:CMP_STRIDE + r + 1, :]).astype(BF16), w1_ref[0, CMP_STRIDE + r])
    z_next = pltpu.roll(z, nc - 1, 0)
    hid = jax.nn.gelu(y + z_next)
    o_ref[0] = _dot(hid.astype(BF16), w2_ref[0])


def _compress(pf, pos, w1, w2):
    t = pf.shape[0]
    nc = t // CMP_STRIDE
    n4 = 2 * NSA_GROUPS
    return pl.pallas_call(
        _compress_body,
        grid=(n4,),
        in_specs=[pl.BlockSpec((t, HEAD_DIM), lambda i: (0, PF_KC + i)),
                  pl.BlockSpec((1, CMP_BLOCK, HEAD_DIM), lambda i: (i // NSA_GROUPS, 0, 0)),
                  pl.BlockSpec((1, CMP_BLOCK, HEAD_DIM, CMP_HIDDEN), lambda i: (i // NSA_GROUPS, 0, 0, 0)),
                  pl.BlockSpec((1, CMP_HIDDEN, HEAD_DIM), lambda i: (i // NSA_GROUPS, 0, 0))],
        out_specs=pl.BlockSpec((1, nc, HEAD_DIM), lambda i: (i, 0, 0)),
        out_shape=jax.ShapeDtypeStruct((n4, nc, HEAD_DIM), F32),
        compiler_params=_cparams(("arbitrary",)),
        name="nsa_compress",
    )(pf, pos, w1, w2)


def _cmp_body(q_ref, kc_ref, vc_ref, ov_ref, o_ref, mb_ref, *, tq, n_sel):
    i = pl.program_id(1)
    nc = kc_ref.shape[1]
    nsp = ov_ref.shape[1]
    qs = _stack_heads(q_ref[...])
    s = _dot_nt(qs, kc_ref[0].astype(BF16))
    t_row = i * tq + (_iota((NSA_REP * tq, 1), 0) & (tq - 1))
    cmp_end = _iota((1, nc), 1) * CMP_STRIDE + (CMP_BLOCK - 1)
    p = _masked_softmax2(s, cmp_end <= t_row)
    o = _dot(p.astype(BF16), vc_ref[0].astype(BF16))
    o_ref[...] = _unstack_heads(o, tq)
    ps = (p[0:tq] + p[tq:2 * tq]) + (p[2 * tq:3 * tq] + p[3 * tq:4 * tq])
    hi, lo = _split2(ps)
    ov = ov_ref[...]
    imp = _dot(hi, ov) + _dot(lo, ov)
    cur = (i * tq + _iota((tq, 1), 0)) >> 6
    blk = _iota((1, nsp), 1)
    valid = blk <= cur
    forced = (blk == 0) | (blk == cur) | (blk == cur - 1)
    score = jnp.where(valid, imp + jnp.where(forced, jnp.inf, 0.0), -jnp.inf)
    blkf = blk.astype(F32)
    sel = jnp.zeros((tq, nsp), F32)
    for _ in range(n_sel):
        m = jnp.max(score, -1, keepdims=True)
        idx = jnp.min(jnp.where(score == m, blkf, float(nsp)), -1, keepdims=True)
        pick = blkf == idx
        sel = jnp.where(pick & (m > -jnp.inf), 1.0, sel)
        score = jnp.where(pick, -jnp.inf, score)
    mb_ref[0] = jnp.where(sel > 0.0, 0.0, NEG).astype(BF16)


def _cmp_select(pb, cc, ov, n_sel):
    t = pb.shape[0]
    nc, nsp = ov.shape
    tq = 128
    return pl.pallas_call(
        functools.partial(_cmp_body, tq=tq, n_sel=n_sel),
        grid=(NSA_GROUPS, t // tq),
        in_specs=[pl.BlockSpec((tq, NSA_REP * HEAD_DIM), lambda g, i: (i, g)),
                  pl.BlockSpec((1, nc, HEAD_DIM), lambda g, i: (g, 0, 0)),
                  pl.BlockSpec((1, nc, HEAD_DIM), lambda g, i: (NSA_GROUPS + g, 0, 0)),
                  pl.BlockSpec((nc, nsp), lambda g, i: (0, 0))],
        out_specs=[pl.BlockSpec((tq, NSA_REP * HEAD_DIM), lambda g, i: (i, g)),
                   pl.BlockSpec((1, tq, nsp), lambda g, i: (g, i, 0))],
        out_shape=[jax.ShapeDtypeStruct((t, NSA_HEADS * HEAD_DIM), F32),
                   jax.ShapeDtypeStruct((NSA_GROUPS, t, nsp), BF16)],
        compiler_params=_cparams(("arbitrary", "arbitrary")),
        name="nsa_cmp_select",
    )(pb, cc, cc, ov)


def _sel_body(qi_ref, ki_ref, qt_ref, mbt_ref, k_ref, e_ref, vt_ref, o_ref, wq_ref, m_ref, acc_ref, *, tq, tk):
    step = pl.program_id(1)
    qi = qi_ref[step]
    ki = ki_ref[step]

    @pl.when(ki == 0)
    def _():
        for h in range(NSA_REP):
            wq_ref[h, 0:HEAD_DIM, :] = qt_ref[h * HEAD_DIM:(h + 1) * HEAD_DIM, :]
        m_ref[...] = jnp.full(m_ref.shape, NEG, F32)
        acc_ref[...] = jnp.zeros(acc_ref.shape, F32)

    @pl.when((ki & (SEL_SUPER // tk - 1)) == 0)
    def _():
        mbt = mbt_ref[0]
        for h in range(NSA_REP):
            wq_ref[h, HEAD_DIM:2 * HEAD_DIM, :] = mbt

    kaug = jnp.concatenate([k_ref[...], e_ref[...]], axis=1)
    vaug = vt_ref[0]

    def update(causal):
        if causal:
            keep = (ki * tk + _iota((tk, 1), 0)) <= (qi * tq + _iota((1, tq), 1))
        for h in range(NSA_REP):
            s = _dot(kaug, wq_ref[h])
            if causal:
                s = jnp.where(keep, s, NEG)
            m_prev = m_ref[h]
            m_new = jnp.maximum(m_prev, jnp.max(s, axis=0, keepdims=True))
            p = jnp.exp2(s - m_new)
            acc_ref[h] = jnp.exp2(m_prev - m_new) * acc_ref[h] + _dot(vaug, p.astype(BF16))
            m_ref[h] = m_new

    crosses_diagonal = ki * tk + (tk - 1) > qi * tq
    pl.when(crosses_diagonal)(functools.partial(update, True))
    pl.when(jnp.logical_not(crosses_diagonal))(functools.partial(update, False))

    @pl.when(ki == ((qi + 1) * tq - 1) // tk)
    def _():
        for h in range(NSA_REP):
            a = acc_ref[h]
            o_ref[:, h * HEAD_DIM:(h + 1) * HEAD_DIM] = (a[0:HEAD_DIM] / a[HEAD_DIM:HEAD_DIM + 1]).T


def _sel_attention(pb, mb, etab):
    t = pb.shape[0]
    tq, tk = 512, 512
    qi, ki = [], []
    for a in range(t // tq):
        for b in range(((a + 1) * tq - 1) // tk + 1):
            qi.append(a)
            ki.append(b)
    nsteps = len(qi)
    sup_tiles = SEL_SUPER // tk
    wq = NSA_HEADS * HEAD_DIM
    vrows = HEAD_DIM + 16
    qt = pb[:, PB_Q * LANES:PB_Q * LANES + wq].T
    mbt = jnp.swapaxes(mb, 1, 2)
    vt = jnp.stack([jnp.concatenate([pb[:, (PB_VS + g) * LANES:(PB_VS + g + 1) * LANES].T,
                                     jnp.ones((vrows - HEAD_DIM, t), pb.dtype)], axis=0)
                    for g in range(NSA_GROUPS)])
    grid_spec = pltpu.PrefetchScalarGridSpec(
        num_scalar_prefetch=2,
        grid=(NSA_GROUPS, nsteps),
        in_specs=[pl.BlockSpec((NSA_REP * HEAD_DIM, tq), lambda g, s, qi, ki: (g, qi[s])),
                  pl.BlockSpec((1, LANES, tq), lambda g, s, qi, ki: (g, ki[s] // sup_tiles, qi[s])),
                  pl.BlockSpec((tk, HEAD_DIM), lambda g, s, qi, ki: (ki[s], PB_KS + g)),
                  pl.BlockSpec((tk, LANES), lambda g, s, qi, ki: (ki[s] % sup_tiles, 0)),
                  pl.BlockSpec((1, vrows, tk), lambda g, s, qi, ki: (g, 0, ki[s]))],
        out_specs=pl.BlockSpec((tq, NSA_REP * HEAD_DIM), lambda g, s, qi, ki: (qi[s], g)),
        scratch_shapes=[pltpu.VMEM((NSA_REP, 2 * HEAD_DIM, tq), BF16),
                        pltpu.VMEM((NSA_REP, 1, tq), F32),
                        pltpu.VMEM((NSA_REP, vrows, tq), F32)],
    )
    return pl.pallas_call(
        functools.partial(_sel_body, tq=tq, tk=tk),
        grid_spec=grid_spec,
        out_shape=jax.ShapeDtypeStruct((t, wq), F32),
        compiler_params=_cparams(("arbitrary", "arbitrary")),
        name="nsa_sel_attn",
    )(jnp.asarray(np.array(qi, np.int32)), jnp.asarray(np.array(ki, np.int32)), qt, mbt, pb, etab, vt)


def _win_body(q_ref, k0_ref, k1_ref, k2_ref, v0_ref, v1_ref, v2_ref, oc_ref, os_ref, sm_ref, g_ref, o_ref, *, tq):
    i = pl.program_id(0)
    t_row = i * tq + (_iota((NSA_REP * tq, 1), 0) & (tq - 1))
    kp = jnp.concatenate([(i - 2 + j) * tq + _iota((1, tq), 1) for j in range(3)], axis=1)
    mask = (kp >= 0) & (kp <= t_row) & (kp > t_row - WINDOW)
    gates = jax.nn.sigmoid(sm_ref[...])
    outs = []
    for g in range(NSA_GROUPS):
        gs = slice(g * HEAD_DIM, (g + 1) * HEAD_DIM)
        qs = _stack_heads(q_ref[:, g * NSA_REP * HEAD_DIM:(g + 1) * NSA_REP * HEAD_DIM])
        k = jnp.concatenate([k0_ref[:, gs], k1_ref[:, gs], k2_ref[:, gs]], axis=0)
        v = jnp.concatenate([v0_ref[:, gs], v1_ref[:, gs], v2_ref[:, gs]], axis=0)
        p = _masked_softmax2(_dot_nt(qs, k), mask)
        ow = _dot(p.astype(BF16), v)
        for h in range(NSA_REP):
            hh = g * NSA_REP + h
            hs = slice(hh * HEAD_DIM, (hh + 1) * HEAD_DIM)
            c = SM_GATE + 3 * hh
            o = (gates[:, c:c + 1] * oc_ref[:, hs] + gates[:, c + 1:c + 2] * os_ref[:, hs]
                 + gates[:, c + 2:c + 3] * ow[h * tq:(h + 1) * tq])
            o = o * lax.rsqrt(jnp.mean(o * o, -1, keepdims=True) + 1e-6) * g_ref[:, hs]
            outs.append(o)
    o_ref[...] = jnp.concatenate(outs, axis=1).astype(o_ref.dtype)


def _win_combine(pb, pf, o_cmp, o_sel, out_g):
    t = pb.shape[0]
    tq = 256
    wq = NSA_HEADS * HEAD_DIM
    wkv = NSA_GROUPS * HEAD_DIM

    def kvspec(j, unit):
        return pl.BlockSpec((tq, wkv), lambda i: (jnp.maximum(i - 2 + j, 0), unit // NSA_GROUPS))

    full = pl.BlockSpec((tq, wq), lambda i: (i, 0))
    return pl.pallas_call(
        functools.partial(_win_body, tq=tq),
        grid=(t // tq,),
        in_specs=[full,
                  kvspec(0, PB_KW), kvspec(1, PB_KW), kvspec(2, PB_KW),
                  kvspec(0, PB_VW), kvspec(1, PB_VW), kvspec(2, PB_VW),
                  full, full,
                  pl.BlockSpec((tq, LANES), lambda i: (i, PF_SMALL)),
                  pl.BlockSpec((1, wq), lambda i: (0, 0))],
        out_specs=full,
        out_shape=jax.ShapeDtypeStruct((t, wq), BF16),
        compiler_params=_cparams(("arbitrary",)),
        name="nsa_win_combine",
    )(pb, pb, pb, pb, pb, pb, pb, o_cmp, o_sel, pf, out_g)


def _gprep_body(x_ref, halo_ref, cw_ref, sm_ref, al_ref, dt_ref, e64_ref, e128_ref,
                qn_ref, kn_ref, v_ref, g64_ref, b64_ref, g128_ref, b128_ref, *, tm):
    i = pl.program_id(0)
    x = x_ref[...]
    halo = halo_ref[...] * (i > 0).astype(F32)
    xe = jnp.concatenate([halo, x], axis=0)
    cw = cw_ref[...]
    y = cw[GDN_CONV - 1:GDN_CONV] * x
    for j in range(GDN_CONV - 1):
        off = 8 - (GDN_CONV - 1) + j
        y = y + cw[j:j + 1] * xe[off:off + tm]
    y = y * jax.nn.sigmoid(y)
    nqk = GDN_QK_HEADS * HEAD_DIM
    for h in range(GDN_QK_HEADS):
        hs = slice(h * HEAD_DIM, (h + 1) * HEAD_DIM)
        q = y[:, hs]
        k = y[:, nqk + h * HEAD_DIM:nqk + (h + 1) * HEAD_DIM]
        qn_ref[:, hs] = q * lax.rsqrt(jnp.sum(q * q, -1, keepdims=True) + 1e-6)
        kn_ref[:, hs] = k * lax.rsqrt(jnp.sum(k * k, -1, keepdims=True) + 1e-6)
    v_ref[...] = y[:, 2 * nqk:]
    sm = sm_ref[...]
    beta = jax.nn.sigmoid(sm)
    z = sm + dt_ref[...]
    softplus = jnp.maximum(z, 0.0) + jnp.log(1.0 + jnp.exp(-jnp.abs(z)))
    g = -jnp.exp(al_ref[...]) * softplus
    g64_ref[...] = _dot_ones_r(g, e64_ref[0])
    b64_ref[...] = _dot_ones_r(beta, e64_ref[1])
    g128_ref[...] = _dot_ones_r(g, e128_ref[0])
    b128_ref[...] = _dot_ones_r(beta, e128_ref[1])


def _head_spread_matrices():
    e64 = np.zeros((2, LANES, GDN_V_HEADS * GDN_CHUNK), np.float32)
    e128 = np.zeros((2, LANES, GDN_V_HEADS * HEAD_DIM), np.float32)
    for kind, base in enumerate((SM_A, SM_BETA)):
        for h in range(GDN_V_HEADS):
            span = (h % 2) * (GDN_V_HEADS // 2) + h // 2
            e64[kind, base + h, span * GDN_CHUNK:(span + 1) * GDN_CHUNK] = 1.0
            e128[kind, base + h, h * HEAD_DIM:(h + 1) * HEAD_DIM] = 1.0
    return jnp.asarray(e64, dtype=BF16), jnp.asarray(e128, dtype=BF16)


def _gdn_prep(pf, conv_w, alog_pad, dt_pad):
    t = pf.shape[0]
    tm = 512
    cch = 2 * GDN_QK_HEADS * HEAD_DIM + GDN_V_HEADS * HEAD_DIM
    nqk = GDN_QK_HEADS * HEAD_DIM
    nv = GDN_V_HEADS * HEAD_DIM
    n64 = GDN_V_HEADS * GDN_CHUNK
    u = PF_QG * LANES // cch
    e64, e128 = _head_spread_matrices()
    lane = pl.BlockSpec((1, LANES), lambda i: (0, 0))
    row = lambda w: pl.BlockSpec((tm, w), lambda i: (i, 0))
    f32 = lambda w: jax.ShapeDtypeStruct((t, w), F32)
    return pl.pallas_call(
        functools.partial(_gprep_body, tm=tm),
        grid=(t // tm,),
        in_specs=[pl.BlockSpec((tm, cch), lambda i: (i, u)),
                  pl.BlockSpec((8, cch), lambda i: (jnp.maximum(i * (tm // 8) - 1, 0), u)),
                  pl.BlockSpec((GDN_CONV, cch), lambda i: (0, 0)),
                  pl.BlockSpec((tm, LANES), lambda i: (i, PF_SMALL)),
                  lane, lane,
                  pl.BlockSpec((2, LANES, n64), lambda i: (0, 0, 0)),
                  pl.BlockSpec((2, LANES, nv), lambda i: (0, 0, 0))],
        out_specs=[row(nqk), row(nqk), row(nv), row(n64), row(n64), row(nv), row(nv)],
        out_shape=[f32(nqk), f32(nqk), f32(nv), f32(n64), f32(n64), f32(nv), f32(nv)],
        compiler_params=_cparams(("arbitrary",)),
        name="gdn_prep",
    )(pf, pf, conv_w, pf, alog_pad, dt_pad, e64, e128)


GDN_PACK = 4


def _packed_unit_lower_inverse(a, eye_p, blk16_p, bd_mask):
    def block_diag(y):
        return jnp.where(bd_mask, jnp.concatenate([y] * GDN_PACK, axis=0), 0.0).astype(BF16)

    def pmm(x, y):
        xh, xl = _split2(x)
        yh = y.astype(BF16)
        yl = y - yh.astype(F32)
        ybh = block_diag(yh.astype(F32))
        return _dot(xh, ybh) + (_dot(xh, block_diag(yl)) + _dot(xl, ybh))

    ad = jnp.where(blk16_p, a, 0.0)
    ao = a - ad
    x2 = pmm(ad, ad)
    x4 = pmm(x2, x2)
    x8 = pmm(x4, x4)
    p = eye_p - ad
    p = p + pmm(p, x2)
    p = p + pmm(p, x4)
    p = p + pmm(p, x8)
    m = pmm(p, ao)
    m2 = pmm(m, m)
    t1 = p + pmm(m2, p)
    return t1 - pmm(m, t1)


def _gscan_body(q_ref, k_ref, v_ref, g64_ref, b64_ref, g128_ref, b128_ref, z_ref, ng_ref, o_ref, s_ref, *, chunks):
    c = GDN_CHUNK
    dh = HEAD_DIM
    n_grp = GDN_V_HEADS // GDN_PACK
    half = GDN_PACK * c

    @pl.when(pl.program_id(0) == 0)
    def _():
        s_ref[...] = jnp.zeros(s_ref.shape, F32)

    r_i = _iota((c, n_grp * half), 0)
    c_i = _iota((c, n_grp * half), 1) & (c - 1)
    tri = r_i >= c_i
    strict = r_i > c_i
    upper_f = (r_i <= c_i).astype(F32)
    r_p = _iota((c, half), 0)
    c_p = _iota((c, half), 1) & (c - 1)
    eye_p = (r_p == c_p).astype(F32)
    blk16_p = (r_p >> 4) == (c_p >> 4)
    bd_mask = (_iota((half, half), 0) >> 6) == (_iota((half, half), 1) >> 6)
    k_mask = (_iota((half, GDN_PACK * dh), 0) >> 6) == (_iota((half, GDN_PACK * dh), 1) >> 7)
    lower = (_iota((c, c), 0) >= _iota((c, c), 1)).astype(BF16)
    ones = jnp.ones((c, c), BF16)
    ng = ng_ref[...]
    zero_blk = jnp.zeros((c, dh), F32)

    def chunk(n, carry):
        rows = pl.ds(pl.multiple_of(n * c, c), c)
        k4 = k_ref[rows, :]
        q4 = q_ref[rows, :] * (dh ** -0.5)
        kbd = jnp.where(k_mask, jnp.concatenate([k4] * GDN_PACK, axis=0), 0.0).astype(BF16)
        kk = _dot_nt(k4.astype(BF16), kbd)
        qk_raw = _dot_nt(q4.astype(BF16), kbd)
        g64 = g64_ref[rows, :]
        gc64 = _dot_ones_l(lower, g64)
        gc_row = _dot_ones_l(ones, g64 * upper_f)
        decay = jnp.where(tri, jnp.exp(jnp.where(tri, gc64 - gc_row, 0.0)), 0.0)
        a_all = jnp.where(strict, jnp.concatenate([kk] * n_grp, axis=1) * decay * b64_ref[rows, :], 0.0)
        qk_all = jnp.where(tri, jnp.concatenate([qk_raw] * n_grp, axis=1) * decay, 0.0)
        gc128 = _dot_ones_l(lower, g128_ref[rows, :])
        egc = jnp.exp(gc128)
        g_last = gc128[c - 1:c]
        k_fade = jnp.exp(g_last - gc128)
        g_tot = jnp.exp(g_last)
        b128 = b128_ref[rows, :]
        v_all = v_ref[rows, :]
        z_all = z_ref[rows, :]
        for grp in range(n_grp):
            def pick(x, grp=grp):
                return jnp.concatenate([x[:, (n_grp * i + grp) * dh:(n_grp * i + grp + 1) * dh]
                                        for i in range(GDN_PACK)], axis=1)

            tinv = _packed_unit_lower_inverse(a_all[:, grp * half:(grp + 1) * half], eye_p, blk16_p, bd_mask)
            beta = pick(b128)
            eg = pick(egc)
            vb = pick(v_all) * beta
            kbe = k4 * beta * eg
            qd = q4 * eg
            kd = k4 * pick(k_fade)
            st = s_ref[grp]
            zero_pair = jnp.zeros((c, 2 * dh), F32)
            rhs_bd = jnp.concatenate(
                [jnp.concatenate([jnp.concatenate([vb[:, i * dh:(i + 1) * dh], kbe[:, i * dh:(i + 1) * dh]], axis=1)
                                  if j == i else zero_pair for j in range(GDN_PACK)], axis=1)
                 for i in range(GDN_PACK)], axis=0)
            th, tl = _split2(tinv)
            rh, rl = _split2(rhs_bd)
            uw_all = _dot(th, rh) + (_dot(th, rl) + _dot(tl, rh))
            r_out, v_new = [], []
            for i in range(GDN_PACK):
                hs = slice(i * dh, (i + 1) * dh)
                u = uw_all[:, 2 * i * dh:(2 * i + 1) * dh]
                w = uw_all[:, (2 * i + 1) * dh:(2 * i + 2) * dh]
                r = _dot(jnp.concatenate([w, qd[:, hs]], axis=0).astype(BF16), st[:, hs].astype(BF16))
                v_new.append(u - r[:c])
                r_out.append(r[c:])
            vn_bd = jnp.concatenate(
                [jnp.concatenate([v_new[i] if j == i else zero_blk for j in range(GDN_PACK)], axis=1)
                 for i in range(GDN_PACK)], axis=0).astype(BF16)
            o_all = jnp.concatenate(r_out, axis=1) + _dot(qk_all[:, grp * half:(grp + 1) * half].astype(BF16), vn_bd)
            k_stack = jnp.concatenate([kd[:, i * dh:(i + 1) * dh] for i in range(GDN_PACK)], axis=0)
            s_ref[grp] = st * pick(g_tot) + _dot(k_stack.T.astype(BF16), vn_bd)
            zg = pick(z_all)
            for i in range(GDN_PACK):
                hs = slice(i * dh, (i + 1) * dh)
                h = n_grp * i + grp
                o = o_all[:, hs]
                zh = zg[:, hs]
                o = o * lax.rsqrt(jnp.mean(o * o, -1, keepdims=True) + 1e-6) * ng * (zh * jax.nn.sigmoid(zh))
                o_ref[rows, h * dh:(h + 1) * dh] = o.astype(o_ref.dtype)
        return carry

    lax.fori_loop(0, chunks, chunk, 0)


def _gdn_scan(qn, kn, v, g64, b64, g128, b128, pf, norm_g):
    t = qn.shape[0]
    chunks = 4
    tm = chunks * GDN_CHUNK
    nqk = GDN_QK_HEADS * HEAD_DIM
    nv = GDN_V_HEADS * HEAD_DIM
    n64 = GDN_V_HEADS * GDN_CHUNK
    row = lambda w: pl.BlockSpec((tm, w), lambda i: (i, 0))
    return pl.pallas_call(
        functools.partial(_gscan_body, chunks=chunks),
        grid=(t // tm,),
        in_specs=[row(nqk), row(nqk), row(nv), row(n64), row(n64), row(nv), row(nv),
                  pl.BlockSpec((tm, nv), lambda i: (i, PF_Z * LANES // nv)),
                  pl.BlockSpec((1, HEAD_DIM), lambda i: (0, 0))],
        out_specs=row(nv),
        out_shape=jax.ShapeDtypeStruct((t, nv), BF16),
        scratch_shapes=[pltpu.VMEM((GDN_V_HEADS // GDN_PACK, HEAD_DIM, GDN_PACK * HEAD_DIM), F32)],
        compiler_params=_cparams(("arbitrary",)),
        name="gdn_scan",
    )(qn, kn, v, g64, b64, g128, b128, pf, norm_g)


def _post_body(on_ref, og_ref, w_ref, x_ref, gm_ref, lg_ref, lb_ref, sh_ref, sc_ref, rw_ref, rb_ref, tril_ref,
               x1_ref, hp_ref, ti_ref, gt_ref, rk_ref, cnt_ref, carry_ref):
    half = D_MODEL // 2

    @pl.when(pl.program_id(0) == 0)
    def _():
        carry_ref[...] = jnp.zeros(carry_ref.shape, F32)

    mix = _dot(on_ref[...], w_ref[0:half, :]) + _dot(og_ref[...], w_ref[half:, :])
    x1 = _ln_plain(DEEPNORM_ALPHA * x_ref[...] + gm_ref[...] * mix, 1e-5) * lg_ref[...] + lb_ref[...]
    x1_ref[...] = x1
    h2 = _ln_plain(x1, 1e-6) * (1.0 + sc_ref[...]) + sh_ref[...]
    lo = lax.bitcast_convert_type(h2[:, :half].astype(BF16).astype(F32), U32) >> 16
    hi = lax.bitcast_convert_type(h2[:, half:].astype(BF16).astype(F32), U32) & jnp.uint32(0xFFFF0000)
    hp_ref[...] = lo | hi
    sc = _dot3(h2, rw_ref[...]) + rb_ref[...]
    lane = _iota((1, LANES), 1).astype(F32)
    hot = jnp.zeros(sc.shape, F32)
    vals, idxs = [], []
    for _ in range(TOP_K):
        m = jnp.max(sc, -1, keepdims=True)
        idx = jnp.min(jnp.where(sc == m, lane, float(LANES)), -1, keepdims=True)
        pick = lane == idx
        hot = jnp.where(pick, 1.0, hot)
        sc = jnp.where(pick, -jnp.inf, sc)
        vals.append(m)
        idxs.append(idx)
    ex = [jnp.exp(v - vals[0]) for v in vals]
    den = (ex[0] + ex[1]) + (ex[2] + ex[3])
    before = carry_ref[...] + _dot(tril_ref[...], hot.astype(BF16))
    ti = jnp.zeros(sc.shape, F32)
    gt = jnp.zeros(sc.shape, F32)
    rk = jnp.zeros(sc.shape, F32)
    for k in range(TOP_K):
        at_k = lane == float(k)
        ti = jnp.where(at_k, idxs[k], ti)
        gt = jnp.where(at_k, ex[k] / den, gt)
        rk = jnp.where(at_k, jnp.sum(jnp.where(lane == idxs[k], before, 0.0), -1, keepdims=True), rk)
    ti_ref[...] = ti.astype(I32)
    gt_ref[...] = gt
    rk_ref[...] = rk.astype(I32)
    carry_ref[...] = carry_ref[...] + jnp.sum(hot, axis=0, keepdims=True)
    cnt_ref[...] = carry_ref[...]


def _post_mixer(o_nsa, o_gdn, w_out, x, gm1, ln_g, ln_b, sh_f, sc_f, rw_pad, rb_pad, tril):
    t, d = x.shape
    tm = tril.shape[0]
    half = d // 2
    vec = pl.BlockSpec((1, d), lambda i: (0, 0))
    lane_vec = pl.BlockSpec((1, LANES), lambda i: (0, 0))
    row = lambda w: pl.BlockSpec((tm, w), lambda i: (i, 0))
    return pl.pallas_call(
        _post_body,
        grid=(t // tm,),
        in_specs=[row(half), row(half), pl.BlockSpec((d, d), lambda i: (0, 0)), row(d),
                  vec, vec, vec, vec, vec,
                  pl.BlockSpec((d, LANES), lambda i: (0, 0)), lane_vec,
                  pl.BlockSpec((tm, tm), lambda i: (0, 0))],
        out_specs=[row(d), row(half), row(LANES), row(LANES), row(LANES), lane_vec],
        out_shape=[jax.ShapeDtypeStruct((t, d), F32), jax.ShapeDtypeStruct((t, half), U32),
                   jax.ShapeDtypeStruct((t, LANES), I32), jax.ShapeDtypeStruct((t, LANES), F32),
                   jax.ShapeDtypeStruct((t, LANES), I32), jax.ShapeDtypeStruct((1, LANES), F32)],
        scratch_shapes=[pltpu.VMEM((1, LANES), F32)],
        compiler_params=_cparams(("arbitrary",)),
        name="post_mixer_router",
    )(o_nsa, o_gdn, w_out, x, gm1, ln_g, ln_b, sh_f, sc_f, rw_pad, rb_pad, tril)


def _row_copy(src_ref, dst_ref, sem):
    return pltpu.make_async_copy(src_ref, dst_ref, sem)


def _dispatch_body(dest_ref, hp_ref, xs_in_ref, xs_ref, sem, *, tm):
    del xs_in_ref

    def issue(r, carry):
        for k in range(TOP_K):
            _row_copy(hp_ref.at[pl.ds(r, 1)], xs_ref.at[pl.ds(dest_ref[r * TOP_K + k], 1)], sem).start()
        return carry

    def drain(r, carry):
        for k in range(TOP_K):
            _row_copy(hp_ref.at[pl.ds(0, 1)], xs_ref.at[pl.ds(0, 1)], sem).wait()
        return carry

    lax.fori_loop(0, tm, issue, 0)
    lax.fori_loop(0, tm, drain, 0)


def _dispatch(dest, hp, rows):
    t, w = hp.shape
    tm = 256
    zeros = jnp.zeros((rows, w), hp.dtype)
    return pl.pallas_call(
        functools.partial(_dispatch_body, tm=tm),
        grid=(t // tm,),
        in_specs=[pl.BlockSpec((tm * TOP_K,), lambda i: (i,), memory_space=pltpu.SMEM),
                  pl.BlockSpec((tm, w), lambda i: (i, 0)),
                  pl.BlockSpec(memory_space=pl.ANY)],
        out_specs=pl.BlockSpec(memory_space=pl.ANY),
        out_shape=jax.ShapeDtypeStruct((rows, w), hp.dtype),
        scratch_shapes=[pltpu.SemaphoreType.DMA(())],
        input_output_aliases={2: 0},
        compiler_params=_cparams(("arbitrary",)),
        name="moe_dispatch",
    )(dest, hp, zeros)


def _deinterleave_body(w_ref, p_ref, g_ref, l_ref):
    perm = p_ref[...]
    for c in range(w_ref.shape[2] // (2 * LANES)):
        r = _dot(w_ref[0, :, c * 2 * LANES:(c + 1) * 2 * LANES].astype(BF16), perm)
        g_ref[0, :, c * LANES:(c + 1) * LANES] = r[:, :LANES].astype(g_ref.dtype)
        l_ref[0, :, c * LANES:(c + 1) * LANES] = r[:, LANES:].astype(l_ref.dtype)


def _split_gate_up(w_gu):
    e, d, n = w_gu.shape
    tr = 512
    perm = np.zeros((2 * LANES, 2 * LANES), np.float32)
    perm[np.arange(0, 2 * LANES, 2), np.arange(LANES)] = 1.0
    perm[np.arange(1, 2 * LANES, 2), LANES + np.arange(LANES)] = 1.0
    out = jax.ShapeDtypeStruct((e, d, n // 2), BF16)
    ospec = pl.BlockSpec((1, tr, n // 2), lambda i, j: (i, j, 0))
    return pl.pallas_call(
        _deinterleave_body,
        grid=(e, d // tr),
        in_specs=[pl.BlockSpec((1, tr, n), lambda i, j: (i, j, 0)),
                  pl.BlockSpec((2 * LANES, 2 * LANES), lambda i, j: (0, 0))],
        out_specs=[ospec, ospec],
        out_shape=[out, out],
        compiler_params=_cparams(("arbitrary", "arbitrary")),
        name="moe_split_gate_up",
    )(w_gu, jnp.asarray(perm, dtype=BF16))


def _gu_body(be_ref, xs_ref, wg_ref, wl_ref, bg_ref, bl_ref, o_ref):
    del be_ref
    half = D_MODEL // 2
    u = xs_ref[...]
    xa = lax.bitcast_convert_type(u << 16, F32).astype(BF16)
    xb = lax.bitcast_convert_type(u & jnp.uint32(0xFFFF0000), F32).astype(BF16)
    tn = 512
    for c in range(D_FF // tn):
        cs = slice(c * tn, (c + 1) * tn)
        hg = _dot(xa, wg_ref[0, 0:half, cs]) + _dot(xb, wg_ref[0, half:, cs]) + bg_ref[0, :, cs]
        hl = _dot(xa, wl_ref[0, 0:half, cs]) + _dot(xb, wl_ref[0, half:, cs]) + bl_ref[0, :, cs]
        x_glu = jnp.minimum(hg, SWIGLU_LIMIT)
        x_lin = jnp.clip(hl, -SWIGLU_LIMIT, SWIGLU_LIMIT)
        o_ref[:, cs] = (x_glu * jax.nn.sigmoid(SWIGLU_ALPHA * x_glu) * (x_lin + 1.0)).astype(o_ref.dtype)


def _expert_gu(blk_exp, xs, wg, wl, bg, bl):
    rows, half = xs.shape
    nb = rows // EXPERT_ROWS
    wspec = pl.BlockSpec((1, D_MODEL, D_FF), lambda b, be: (be[b], 0, 0))
    bspec = pl.BlockSpec((1, 1, D_FF), lambda b, be: (be[b], 0, 0))
    grid_spec = pltpu.PrefetchScalarGridSpec(
        num_scalar_prefetch=1,
        grid=(nb,),
        in_specs=[pl.BlockSpec((EXPERT_ROWS, half), lambda b, be: (b, 0)), wspec, wspec, bspec, bspec],
        out_specs=pl.BlockSpec((EXPERT_ROWS, D_FF), lambda b, be: (b, 0)),
    )
    return pl.pallas_call(
        _gu_body,
        grid_spec=grid_spec,
        out_shape=jax.ShapeDtypeStruct((rows, D_FF), BF16),
        compiler_params=_cparams(("arbitrary",)),
        name="moe_gate_up",
    )(blk_exp, xs, wg, wl, bg, bl)


def _dn_body(be_ref, h_ref, w_ref, b_ref, o_ref):
    del be_ref
    o_ref[...] = _dot(h_ref[...], w_ref[0]) + b_ref[0]


def _expert_dn(blk_exp, hact, wd, bd):
    rows = hact.shape[0]
    nb = rows // EXPERT_ROWS
    grid_spec = pltpu.PrefetchScalarGridSpec(
        num_scalar_prefetch=1,
        grid=(nb,),
        in_specs=[pl.BlockSpec((EXPERT_ROWS, D_FF), lambda b, be: (b, 0)),
                  pl.BlockSpec((1, D_FF, D_MODEL), lambda b, be: (be[b], 0, 0)),
                  pl.BlockSpec((1, 1, D_MODEL), lambda b, be: (be[b], 0, 0))],
        out_specs=pl.BlockSpec((EXPERT_ROWS, D_MODEL), lambda b, be: (b, 0)),
    )
    return pl.pallas_call(
        _dn_body,
        grid_spec=grid_spec,
        out_shape=jax.ShapeDtypeStruct((rows, D_MODEL), F32),
        compiler_params=_cparams(("arbitrary",)),
        name="moe_down",
    )(blk_exp, hact, wd, bd)


def _combine_body(dest_ref, y_ref, gt_ref, x1_ref, gf_ref, lg_ref, lb_ref, o_ref, buf_ref, sem, *, tm):
    def issue(r, carry):
        for k in range(TOP_K):
            _row_copy(y_ref.at[pl.ds(dest_ref[r * TOP_K + k], 1)], buf_ref.at[k, pl.ds(r, 1)], sem).start()
        return carry

    def drain(r, carry):
        for k in range(TOP_K):
            _row_copy(y_ref.at[pl.ds(0, 1)], buf_ref.at[k, pl.ds(0, 1)], sem).wait()
        return carry

    lax.fori_loop(0, tm, issue, 0)
    lax.fori_loop(0, tm, drain, 0)
    gt = gt_ref[...]
    y = (gt[:, 0:1] * buf_ref[0] + gt[:, 1:2] * buf_ref[1]) + (gt[:, 2:3] * buf_ref[2] + gt[:, 3:4] * buf_ref[3])
    o_ref[...] = _ln_plain(DEEPNORM_ALPHA * x1_ref[...] + gf_ref[...] * y, 1e-5) * lg_ref[...] + lb_ref[...]


def _combine(dest, y, gt, x1, gf1, ln_g, ln_b):
    t, d = x1.shape
    tm = 256
    vec = pl.BlockSpec((1, d), lambda i: (0, 0))
    return pl.pallas_call(
        functools.partial(_combine_body, tm=tm),
        grid=(t // tm,),
        in_specs=[pl.BlockSpec((tm * TOP_K,), lambda i: (i,), memory_space=pltpu.SMEM),
                  pl.BlockSpec(memory_space=pl.ANY),
                  pl.BlockSpec((tm, LANES), lambda i: (i, 0)),
                  pl.BlockSpec((tm, d), lambda i: (i, 0)), vec, vec, vec],
        out_specs=pl.BlockSpec((tm, d), lambda i: (i, 0)),
        out_shape=jax.ShapeDtypeStruct((t, d), F32),
        scratch_shapes=[pltpu.VMEM((TOP_K, tm, d), F32), pltpu.SemaphoreType.DMA(())],
        compiler_params=_cparams(("arbitrary",)),
        name="moe_combine",
    )(dest, y, gt, x1, gf1, ln_g, ln_b)


def _overlap_matrix(t):
    nc_pad = t // CMP_STRIDE
    n_c = (t - CMP_BLOCK) // CMP_STRIDE + 1
    n_s = t // SEL_BLOCK
    nsp = -(-n_s // LANES) * LANES
    ci = np.arange(nc_pad)[:, None] * CMP_STRIDE
    sj = np.arange(nsp)[None, :] * SEL_BLOCK
    ov = (ci < sj + SEL_BLOCK) & (ci + CMP_BLOCK > sj)
    ov &= (np.arange(nc_pad)[:, None] < n_c) & (np.arange(nsp)[None, :] < n_s)
    return jnp.asarray(ov.astype(np.float32), dtype=BF16)


def _block_onehot():
    e = (np.arange(SEL_SUPER)[:, None] // SEL_BLOCK) == np.arange(LANES)[None, :]
    return jnp.asarray(e.astype(np.float32), dtype=BF16)


def _in_proj_weights(w_in):
    nq = NSA_HEADS * HEAD_DIM
    nkv = NSA_GROUPS * HEAD_DIM
    o = np.cumsum([0, nq, nkv, nkv, nkv, nkv, nkv, nkv, 3 * NSA_HEADS,
                   GDN_QK_HEADS * HEAD_DIM, GDN_QK_HEADS * HEAD_DIM, GDN_V_HEADS * HEAD_DIM,
                   GDN_V_HEADS * HEAD_DIM, GDN_V_HEADS, GDN_V_HEADS]).tolist()
    q, kc, vc, ks, vs, kw, vw, gate, qg, kg, vg, z, b_l, a_l = [w_in[:, o[i]:o[i + 1]] for i in range(14)]
    wb = jnp.concatenate([q, ks, vs, kw, vw], axis=1).astype(BF16)
    small = jnp.concatenate([gate, b_l, a_l], axis=1)
    pad = jnp.zeros((w_in.shape[0], 4 * LANES - small.shape[1]), w_in.dtype)
    wf = jnp.concatenate([qg, kg, vg, z, kc, vc, small, pad], axis=1).astype(BF16)
    cs_b = jnp.concatenate([jnp.full((1, nq), HEAD_DIM ** -0.5 * LOG2E, F32), jnp.ones((1, wb.shape[1] - nq), F32)],
                           axis=1)
    cs_f = jnp.ones((1, wf.shape[1]), F32)
    return wb, cs_b, wf, cs_f


def _lane_pad(v, start):
    return jnp.zeros((1, LANES), F32).at[0, start:start + v.shape[0]].set(v)


def _layer(x, mod, w_in, cmp_pos, cmp_w1, cmp_w2, nsa_out_g, conv_w, a_log, dt_bias, gdn_norm_g, w_out,
           ln_mix_g, ln_mix_b, router_w, router_b, w_gu, b_gu, w_dn, b_dn, ln_ffn_g, ln_ffn_b, consts):
    t, d = x.shape
    ov, etab, tril = consts
    sh_m, sc_m, g_m, sh_f, sc_f, g_f = [mod[:, i * d:(i + 1) * d] for i in range(6)]

    h = _lnmod(x, sh_m, sc_m)
    wb, cs_b, wf, cs_f = _in_proj_weights(w_in)
    pb = _matmul(h, wb, cs_b, BF16)
    pf = _matmul(h, wf, cs_f, F32)
    cc = _compress(pf, cmp_pos, cmp_w1, cmp_w2)
    n_sel = min(SEL_COUNT, t // SEL_BLOCK)
    o_cmp, mb = _cmp_select(pb, cc, ov, n_sel)
    o_sel = _sel_attention(pb, mb, etab)
    o_nsa = _win_combine(pb, pf, o_cmp, o_sel, nsa_out_g.reshape(1, -1))
    qn, kn, v, g64, b64, g128, b128 = _gdn_prep(pf, conv_w.reshape(GDN_CONV, -1), _lane_pad(a_log, SM_A),
                                                _lane_pad(dt_bias, SM_A))
    o_gdn = _gdn_scan(qn, kn, v, g64, b64, g128, b128, pf, gdn_norm_g.reshape(1, -1))

    rw_pad = jnp.zeros((d, LANES), F32).at[:, :N_EXPERTS].set(router_w)
    rb_pad = jnp.full((1, LANES), -jnp.inf, F32).at[0, :N_EXPERTS].set(router_b)
    x1, hp, ti, gt, rk, cnt = _post_mixer(o_nsa, o_gdn, w_out.astype(BF16), x, 1.0 + g_m, ln_mix_g.reshape(1, d),
                                           ln_mix_b.reshape(1, d), sh_f, sc_f, rw_pad, rb_pad, tril)

    counts = cnt[0, :N_EXPERTS].astype(I32)
    padded = (counts + EXPERT_ROWS - 1) // EXPERT_ROWS * EXPERT_ROWS
    pad_end = jnp.cumsum(padded)
    pad_start = pad_end - padded
    dest = (pad_start[ti[:, :TOP_K]] + rk[:, :TOP_K]).reshape(-1)
    nb = t * TOP_K // EXPERT_ROWS + N_EXPERTS
    blk_start = jnp.arange(nb, dtype=I32) * EXPERT_ROWS
    blk_exp = jnp.minimum(jnp.sum((pad_end[None, :] <= blk_start[:, None]).astype(I32), axis=1), N_EXPERTS - 1)

    xs = _dispatch(dest, hp, nb * EXPERT_ROWS)
    wg, wl = _split_gate_up(w_gu)
    bg = b_gu[:, None, 0::2]
    bl = b_gu[:, None, 1::2]
    hact = _expert_gu(blk_exp, xs, wg, wl, bg, bl)
    y = _expert_dn(blk_exp, hact, w_dn.astype(BF16), b_dn[:, None, :])
    return _combine(dest, y, gt, x1, 1.0 + g_f, ln_ffn_g.reshape(1, d), ln_ffn_b.reshape(1, d))


def kernel(x, c, w_in, cmp_pos_k, cmp_pos_v, cmp_k_w1, cmp_k_w2, cmp_v_w1, cmp_v_w2, nsa_out_g, gdn_conv_w,
           gdn_a_log, gdn_dt_bias, gdn_norm_g, w_out, w_ada, b_ada, ln_mix_g, ln_mix_b, router_w, router_b,
           w_gu, b_gu, w_dn, b_dn, ln_ffn_g, ln_ffn_b):
    bsz, t, d = x.shape
    assert bsz == 1 and d == D_MODEL and t % 1024 == 0
    nl = w_in.shape[0]
    consts = (_overlap_matrix(t), _block_onehot(),
              jnp.asarray(np.tril(np.ones((512, 512), np.float32), -1), dtype=BF16))
    mod = _ada(c, w_ada, b_ada)
    xt = x.reshape(t, d)
    for l in range(nl):
        pos = jnp.stack([cmp_pos_k[l], cmp_pos_v[l]])
        w1 = jnp.stack([cmp_k_w1[l], cmp_v_w1[l]]).reshape(2, CMP_BLOCK, HEAD_DIM, CMP_HIDDEN).astype(BF16)
        w2 = jnp.stack([cmp_k_w2[l], cmp_v_w2[l]]).astype(BF16)
        xt = _layer(xt, mod[l], w_in[l], pos, w1, w2, nsa_out_g[l], gdn_conv_w[l], gdn_a_log[l], gdn_dt_bias[l],
                    gdn_norm_g[l], w_out[l], ln_mix_g[l], ln_mix_b[l], router_w[l], router_b[l], w_gu[l], b_gu[l],
                    w_dn[l], b_dn[l], ln_ffn_g[l], ln_ffn_b[l], consts)
    return xt.reshape(bsz, t, d)
```

```python
import functools

import numpy as np
import jax
import jax.numpy as jnp
from jax import lax
from jax.experimental import pallas as pl
from jax.experimental.pallas import tpu as pltpu

F32 = jnp.float32
BF16 = jnp.bfloat16
I32 = jnp.int32
U32 = jnp.uint32

D_MODEL = 2048
DEPTH = 2
HEAD_DIM = 128
NSA_HEADS = 8
NSA_GROUPS = 2
NSA_REP = NSA_HEADS // NSA_GROUPS
CMP_BLOCK = 32
CMP_STRIDE = 16
CMP_HIDDEN = 256
SEL_BLOCK = 64
SEL_COUNT = 16
WINDOW = 512
GDN_QK_HEADS = 4
GDN_V_HEADS = 8
GDN_CONV = 4
GDN_CHUNK = 64
N_EXPERTS = 32
TOP_K = 4
D_FF = 2048
SWIGLU_LIMIT = 7.0
SWIGLU_ALPHA = 1.702
DEEPNORM_ALPHA = (2 * DEPTH) ** 0.25

LANES = 128
VMEM_LIMIT = 56 * 1024 * 1024
NEG = -1e30
LOG2E = 1.4426950408889634
SEL_SUPER = LANES * SEL_BLOCK
EXPERT_ROWS = 512

PB_Q, PB_KS, PB_VS, PB_KW, PB_VW = 0, 8, 10, 12, 14
PF_QG, PF_KG, PF_VG, PF_Z, PF_KC, PF_VC, PF_SMALL = 0, 4, 8, 16, 24, 26, 28
SM_GATE, SM_BETA, SM_A = 0, 24, 32


def _cparams(sem):
    return pltpu.CompilerParams(dimension_semantics=sem, vmem_limit_bytes=VMEM_LIMIT)


def _dot(a, b):
    return jnp.dot(a, b, preferred_element_type=F32)


def _dot_nt(a, b):
    return lax.dot_general(a, b, (((1,), (1,)), ((), ())), preferred_element_type=F32)


def _split2(a):
    hi = a.astype(BF16)
    return hi, (a - hi.astype(F32)).astype(BF16)


def _split3(a):
    hi = a.astype(BF16)
    r = a - hi.astype(F32)
    mid = r.astype(BF16)
    return hi, mid, (r - mid.astype(F32)).astype(BF16)


def _dot3(a, b):
    ah, al = _split2(a)
    bh, bl = _split2(b)
    return _dot(ah, bh) + (_dot(ah, bl) + _dot(al, bh))


def _dot_ones_l(ones_bf16, x):
    h, m, l = _split3(x)
    return _dot(ones_bf16, h) + (_dot(ones_bf16, m) + _dot(ones_bf16, l))


def _dot_ones_r(x, ones_bf16):
    h, m, l = _split3(x)
    return _dot(h, ones_bf16) + (_dot(m, ones_bf16) + _dot(l, ones_bf16))


def _ln_plain(x, eps):
    mu = jnp.mean(x, -1, keepdims=True)
    xc = x - mu
    var = jnp.mean(xc * xc, -1, keepdims=True)
    return xc * lax.rsqrt(var + eps)


def _masked_softmax2(s2, mask):
    s2 = jnp.where(mask, s2, -jnp.inf)
    m = jnp.max(s2, -1, keepdims=True)
    m = jnp.where(jnp.abs(m) < jnp.inf, m, 0.0)
    p = jnp.exp2(s2 - m)
    return p / jnp.maximum(jnp.sum(p, -1, keepdims=True), 1e-30)


def _iota(shape, dim):
    return lax.broadcasted_iota(I32, shape, dim)


def _stack_heads(q):
    return jnp.concatenate([q[:, h * HEAD_DIM:(h + 1) * HEAD_DIM] for h in range(NSA_REP)], axis=0)


def _unstack_heads(o, t):
    return jnp.concatenate([o[h * t:(h + 1) * t] for h in range(NSA_REP)], axis=1)


def _ada_body(c_ref, w_ref, b_ref, o_ref):
    c = c_ref[...]
    ca = c * jax.nn.sigmoid(c)
    o_ref[0] = jnp.sum(w_ref[0] * ca, axis=0, keepdims=True) + b_ref[0]


def _ada(c, w_ada, b_ada):
    nl, d, n = w_ada.shape
    tn = 1024
    return pl.pallas_call(
        _ada_body,
        grid=(nl, n // tn),
        in_specs=[pl.BlockSpec((d, 1), lambda l, j: (0, 0)),
                  pl.BlockSpec((1, d, tn), lambda l, j: (l, 0, j)),
                  pl.BlockSpec((1, 1, tn), lambda l, j: (l, 0, j))],
        out_specs=pl.BlockSpec((1, 1, tn), lambda l, j: (l, 0, j)),
        out_shape=jax.ShapeDtypeStruct((nl, 1, n), F32),
        compiler_params=_cparams(("arbitrary", "arbitrary")),
        name="ada_mod",
    )(c.reshape(d, 1), w_ada, b_ada.reshape(nl, 1, n))


def _lnmod_body(x_ref, sh_ref, sc_ref, o_ref):
    h = _ln_plain(x_ref[...], 1e-6) * (1.0 + sc_ref[...]) + sh_ref[...]
    o_ref[...] = h.astype(o_ref.dtype)


def _lnmod(x, shift, scale):
    t, d = x.shape
    tm = 512
    vec = pl.BlockSpec((1, d), lambda i: (0, 0))
    return pl.pallas_call(
        _lnmod_body,
        grid=(t // tm,),
        in_specs=[pl.BlockSpec((tm, d), lambda i: (i, 0)), vec, vec],
        out_specs=pl.BlockSpec((tm, d), lambda i: (i, 0)),
        out_shape=jax.ShapeDtypeStruct((t, d), BF16),
        compiler_params=_cparams(("arbitrary",)),
        name="ln_mod",
    )(x, shift, scale)


def _mm_body(a_ref, w_ref, cs_ref, o_ref):
    o_ref[...] = (_dot(a_ref[...], w_ref[...]) * cs_ref[...]).astype(o_ref.dtype)


def _matmul(a, w, colscale, out_dtype):
    m, k = a.shape
    n = w.shape[1]
    tm, tn = 1024, 512
    return pl.pallas_call(
        _mm_body,
        grid=(m // tm, n // tn),
        in_specs=[pl.BlockSpec((tm, k), lambda i, j: (i, 0)),
                  pl.BlockSpec((k, tn), lambda i, j: (0, j)),
                  pl.BlockSpec((1, tn), lambda i, j: (0, j))],
        out_specs=pl.BlockSpec((tm, tn), lambda i, j: (i, j)),
        out_shape=jax.ShapeDtypeStruct((m, n), out_dtype),
        compiler_params=_cparams(("arbitrary", "arbitrary")),
        name="proj_mm",
    )(a, w, colscale)


def _compress_body(x_ref, p_ref, w1_ref, w2_ref, o_ref):
    nc = o_ref.shape[1]
    y = jnp.zeros((nc, CMP_HIDDEN), F32)
    z = jnp.zeros((nc, CMP_HIDDEN), F32)
    for r in range(CMP_STRIDE):
        xr = x_ref[pl.ds(r, nc, stride=CMP_STRIDE), :]
        y = y + _dot((xr + p_ref[0, r:r + 1, :]).astype(BF16), w1_ref[0, r])
        z = z + _dot((xr + p_ref[0, CMP_STRIDE + r:CMP_STRIDE + r + 1, :]).astype(BF16), w1_ref[0, CMP_STRIDE + r])
    z_next = pltpu.roll(z, nc - 1, 0)
    hid = jax.nn.gelu(y + z_next)
    o_ref[0] = _dot(hid.astype(BF16), w2_ref[0])


def _compress(pf, pos, w1, w2):
    t = pf.shape[0]
    nc = t // CMP_STRIDE
    n4 = 2 * NSA_GROUPS
    return pl.pallas_call(
        _compress_body,
        grid=(n4,),
        in_specs=[pl.BlockSpec((t, HEAD_DIM), lambda i: (0, PF_KC + i)),
                  pl.BlockSpec((1, CMP_BLOCK, HEAD_DIM), lambda i: (i // NSA_GROUPS, 0, 0)),
                  pl.BlockSpec((1, CMP_BLOCK, HEAD_DIM, CMP_HIDDEN), lambda i: (i // NSA_GROUPS, 0, 0, 0)),
                  pl.BlockSpec((1, CMP_HIDDEN, HEAD_DIM), lambda i: (i // NSA_GROUPS, 0, 0))],
        out_specs=pl.BlockSpec((1, nc, HEAD_DIM), lambda i: (i, 0, 0)),
        out_shape=jax.ShapeDtypeStruct((n4, nc, HEAD_DIM), F32),
        compiler_params=_cparams(("arbitrary",)),
        name="nsa_compress",
    )(pf, pos, w1, w2)


def _cmp_body(q_ref, kc_ref, vc_ref, ov_ref, o_ref, mb_ref, *, tq, n_sel):
    i = pl.program_id(1)
    nc = kc_ref.shape[1]
    nsp = ov_ref.shape[1]
    qs = _stack_heads(q_ref[...])
    s = _dot_nt(qs, kc_ref[0].astype(BF16))
    t_row = i * tq + (_iota((NSA_REP * tq, 1), 0) & (tq - 1))
    cmp_end = _iota((1, nc), 1) * CMP_STRIDE + (CMP_BLOCK - 1)
    p = _masked_softmax2(s, cmp_end <= t_row)
    o = _dot(p.astype(BF16), vc_ref[0].astype(BF16))
    o_ref[...] = _unstack_heads(o, tq)
    ps = (p[0:tq] + p[tq:2 * tq]) + (p[2 * tq:3 * tq] + p[3 * tq:4 * tq])
    hi, lo = _split2(ps)
    ov = ov_ref[...]
    imp = (_dot(hi, ov) + _dot(lo, ov)).T
    cur = (i * tq + _iota((1, tq), 1)) >> 6
    blk = _iota((nsp, 1), 0)
    valid = blk <= cur
    forced = (blk == 0) | (blk == cur) | (blk == cur - 1)
    score = jnp.where(valid, imp + jnp.where(forced, jnp.inf, 0.0), -jnp.inf)
    blkf = blk.astype(F32)
    sel = jnp.zeros((nsp, tq), F32)
    for _ in range(n_sel):
        m = jnp.max(score, axis=0, keepdims=True)
        idx = jnp.min(jnp.where(score == m, blkf, float(nsp)), axis=0, keepdims=True)
        pick = blkf == idx
        sel = jnp.where(pick & (m > -jnp.inf), 1.0, sel)
        score = jnp.where(pick, -jnp.inf, score)
    mb_ref[0] = jnp.where(sel > 0.0, 0.0, NEG).astype(BF16)


def _cmp_select(pb, cc, ov, n_sel):
    t = pb.shape[0]
    nc, nsp = ov.shape
    tq = 128
    return pl.pallas_call(
        functools.partial(_cmp_body, tq=tq, n_sel=n_sel),
        grid=(NSA_GROUPS, t // tq),
        in_specs=[pl.BlockSpec((tq, NSA_REP * HEAD_DIM), lambda g, i: (i, g)),
                  pl.BlockSpec((1, nc, HEAD_DIM), lambda g, i: (g, 0, 0)),
                  pl.BlockSpec((1, nc, HEAD_DIM), lambda g, i: (NSA_GROUPS + g, 0, 0)),
                  pl.BlockSpec((nc, nsp), lambda g, i: (0, 0))],
        out_specs=[pl.BlockSpec((tq, NSA_REP * HEAD_DIM), lambda g, i: (i, g)),
                   pl.BlockSpec((1, nsp, tq), lambda g, i: (g, 0, i))],
        out_shape=[jax.ShapeDtypeStruct((t, NSA_HEADS * HEAD_DIM), F32),
                   jax.ShapeDtypeStruct((NSA_GROUPS, nsp, t), BF16)],
        compiler_params=_cparams(("arbitrary", "arbitrary")),
        name="nsa_cmp_select",
    )(pb, cc, cc, ov)


def _sel_body(qi_ref, ki_ref, qt_ref, mbt_ref, k_ref, e_ref, vt_ref, o_ref, wq_ref, m_ref, acc_ref, *, tq, tk):
    step = pl.program_id(1)
    qi = qi_ref[step]
    ki = ki_ref[step]

    @pl.when(ki == 0)
    def _():
        for h in range(NSA_REP):
            wq_ref[h, 0:HEAD_DIM, :] = qt_ref[h * HEAD_DIM:(h + 1) * HEAD_DIM, :]
        m_ref[...] = jnp.full(m_ref.shape, NEG, F32)
        acc_ref[...] = jnp.zeros(acc_ref.shape, F32)

    @pl.when((ki & (SEL_SUPER // tk - 1)) == 0)
    def _():
        mbt = mbt_ref[0]
        for h in range(NSA_REP):
            wq_ref[h, HEAD_DIM:2 * HEAD_DIM, :] = mbt

    kaug = jnp.concatenate([k_ref[...], e_ref[...]], axis=1)
    vaug = vt_ref[0]

    def update(causal):
        if causal:
            keep = (ki * tk + _iota((tk, 1), 0)) <= (qi * tq + _iota((1, tq), 1))
        for h in range(NSA_REP):
            s = _dot(kaug, wq_ref[h])
            if causal:
                s = jnp.where(keep, s, NEG)
            m_prev = m_ref[h]
            m_new = jnp.maximum(m_prev, jnp.max(s, axis=0, keepdims=True))
            p = jnp.exp2(s - m_new)
            acc_ref[h] = jnp.exp2(m_prev - m_new) * acc_ref[h] + _dot(vaug, p.astype(BF16))
            m_ref[h] = m_new

    crosses_diagonal = ki * tk + (tk - 1) > qi * tq
    pl.when(crosses_diagonal)(functools.partial(update, True))
    pl.when(jnp.logical_not(crosses_diagonal))(functools.partial(update, False))

    @pl.when(ki == ((qi + 1) * tq - 1) // tk)
    def _():
        for h in range(NSA_REP):
            a = acc_ref[h]
            o_ref[:, h * HEAD_DIM:(h + 1) * HEAD_DIM] = (a[0:HEAD_DIM] / a[HEAD_DIM:HEAD_DIM + 1]).T


def _sel_attention(pb, mbt, etab):
    t = pb.shape[0]
    tq, tk = 512, 512
    qi, ki = [], []
    for a in range(t // tq):
        for b in range(((a + 1) * tq - 1) // tk + 1):
            qi.append(a)
            ki.append(b)
    nsteps = len(qi)
    sup_tiles = SEL_SUPER // tk
    wq = NSA_HEADS * HEAD_DIM
    vrows = HEAD_DIM + 16
    qt = pb[:, PB_Q * LANES:PB_Q * LANES + wq].T
    vt = jnp.stack([jnp.concatenate([pb[:, (PB_VS + g) * LANES:(PB_VS + g + 1) * LANES].T,
                                     jnp.ones((vrows - HEAD_DIM, t), pb.dtype)], axis=0)
                    for g in range(NSA_GROUPS)])
    grid_spec = pltpu.PrefetchScalarGridSpec(
        num_scalar_prefetch=2,
        grid=(NSA_GROUPS, nsteps),
        in_specs=[pl.BlockSpec((NSA_REP * HEAD_DIM, tq), lambda g, s, qi, ki: (g, qi[s])),
                  pl.BlockSpec((1, LANES, tq), lambda g, s, qi, ki: (g, ki[s] // sup_tiles, qi[s])),
                  pl.BlockSpec((tk, HEAD_DIM), lambda g, s, qi, ki: (ki[s], PB_KS + g)),
                  pl.BlockSpec((tk, LANES), lambda g, s, qi, ki: (ki[s] % sup_tiles, 0)),
                  pl.BlockSpec((1, vrows, tk), lambda g, s, qi, ki: (g, 0, ki[s]))],
        out_specs=pl.BlockSpec((tq, NSA_REP * HEAD_DIM), lambda g, s, qi, ki: (qi[s], g)),
        scratch_shapes=[pltpu.VMEM((NSA_REP, 2 * HEAD_DIM, tq), BF16),
                        pltpu.VMEM((NSA_REP, 1, tq), F32),
                        pltpu.VMEM((NSA_REP, vrows, tq), F32)],
    )
    return pl.pallas_call(
        functools.partial(_sel_body, tq=tq, tk=tk),
        grid_spec=grid_spec,
        out_shape=jax.ShapeDtypeStruct((t, wq), F32),
        compiler_params=_cparams(("arbitrary", "arbitrary")),
        name="nsa_sel_attn",
    )(jnp.asarray(np.array(qi, np.int32)), jnp.asarray(np.array(ki, np.int32)), qt, mbt, pb, etab, vt)


def _win_body(q_ref, k0_ref, k1_ref, k2_ref, v0_ref, v1_ref, v2_ref, oc_ref, os_ref, sm_ref, g_ref, o_ref, *, tq):
    i = pl.program_id(0)
    t_row = i * tq + (_iota((NSA_REP * tq, 1), 0) & (tq - 1))
    kp = jnp.concatenate([(i - 2 + j) * tq + _iota((1, tq), 1) for j in range(3)], axis=1)
    mask = (kp >= 0) & (kp <= t_row) & (kp > t_row - WINDOW)
    gates = jax.nn.sigmoid(sm_ref[...])
    outs = []
    for g in range(NSA_GROUPS):
        gs = slice(g * HEAD_DIM, (g + 1) * HEAD_DIM)
        qs = _stack_heads(q_ref[:, g * NSA_REP * HEAD_DIM:(g + 1) * NSA_REP * HEAD_DIM])
        k = jnp.concatenate([k0_ref[:, gs], k1_ref[:, gs], k2_ref[:, gs]], axis=0)
        v = jnp.concatenate([v0_ref[:, gs], v1_ref[:, gs], v2_ref[:, gs]], axis=0)
        p = _masked_softmax2(_dot_nt(qs, k), mask)
        ow = _dot(p.astype(BF16), v)
        for h in range(NSA_REP):
            hh = g * NSA_REP + h
            hs = slice(hh * HEAD_DIM, (hh + 1) * HEAD_DIM)
            c = SM_GATE + 3 * hh
            o = (gates[:, c:c + 1] * oc_ref[:, hs] + gates[:, c + 1:c + 2] * os_ref[:, hs]
                 + gates[:, c + 2:c + 3] * ow[h * tq:(h + 1) * tq])
            o = o * lax.rsqrt(jnp.mean(o * o, -1, keepdims=True) + 1e-6) * g_ref[:, hs]
            outs.append(o)
    o_ref[...] = jnp.concatenate(outs, axis=1).astype(o_ref.dtype)


def _win_combine(pb, pf, o_cmp, o_sel, out_g):
    t = pb.shape[0]
    tq = 256
    wq = NSA_HEADS * HEAD_DIM
    wkv = NSA_GROUPS * HEAD_DIM

    def kvspec(j, unit):
        return pl.BlockSpec((tq, wkv), lambda i: (jnp.maximum(i - 2 + j, 0), unit // NSA_GROUPS))

    full = pl.BlockSpec((tq, wq), lambda i: (i, 0))
    return pl.pallas_call(
        functools.partial(_win_body, tq=tq),
        grid=(t // tq,),
        in_specs=[full,
                  kvspec(0, PB_KW), kvspec(1, PB_KW), kvspec(2, PB_KW),
                  kvspec(0, PB_VW), kvspec(1, PB_VW), kvspec(2, PB_VW),
                  full, full,
                  pl.BlockSpec((tq, LANES), lambda i: (i, PF_SMALL)),
                  pl.BlockSpec((1, wq), lambda i: (0, 0))],
        out_specs=full,
        out_shape=jax.ShapeDtypeStruct((t, wq), BF16),
        compiler_params=_cparams(("arbitrary",)),
        name="nsa_win_combine",
    )(pb, pb, pb, pb, pb, pb, pb, o_cmp, o_sel, pf, out_g)


def _gprep_body(x_ref, halo_ref, cw_ref, sm_ref, al_ref, dt_ref, e64_ref, e128_ref,
                qn_ref, kn_ref, v_ref, g64_ref, b64_ref, g128_ref, b128_ref, *, tm):
    i = pl.program_id(0)
    x = x_ref[...]
    halo = halo_ref[...] * (i > 0).astype(F32)
    xe = jnp.concatenate([halo, x], axis=0)
    cw = cw_ref[...]
    y = cw[GDN_CONV - 1:GDN_CONV] * x
    for j in range(GDN_CONV - 1):
        off = 8 - (GDN_CONV - 1) + j
        y = y + cw[j:j + 1] * xe[off:off + tm]
    y = y * jax.nn.sigmoid(y)
    nqk = GDN_QK_HEADS * HEAD_DIM
    for h in range(GDN_QK_HEADS):
        hs = slice(h * HEAD_DIM, (h + 1) * HEAD_DIM)
        q = y[:, hs]
        k = y[:, nqk + h * HEAD_DIM:nqk + (h + 1) * HEAD_DIM]
        qn_ref[:, hs] = q * lax.rsqrt(jnp.sum(q * q, -1, keepdims=True) + 1e-6)
        kn_ref[:, hs] = k * lax.rsqrt(jnp.sum(k * k, -1, keepdims=True) + 1e-6)
    v_ref[...] = y[:, 2 * nqk:]
    sm = sm_ref[...]
    beta = jax.nn.sigmoid(sm)
    z = sm + dt_ref[...]
    softplus = jnp.maximum(z, 0.0) + jnp.log(1.0 + jnp.exp(-jnp.abs(z)))
    g = -jnp.exp(al_ref[...]) * softplus
    g64_ref[...] = _dot_ones_r(g, e64_ref[0])
    b64_ref[...] = _dot_ones_r(beta, e64_ref[1])
    g128_ref[...] = _dot_ones_r(g, e128_ref[0])
    b128_ref[...] = _dot_ones_r(beta, e128_ref[1])


def _head_spread_matrices():
    e64 = np.zeros((2, LANES, GDN_V_HEADS * GDN_CHUNK), np.float32)
    e128 = np.zeros((2, LANES, GDN_V_HEADS * HEAD_DIM), np.float32)
    for kind, base in enumerate((SM_A, SM_BETA)):
        for h in range(GDN_V_HEADS):
            span = (h % 2) * (GDN_V_HEADS // 2) + h // 2
            e64[kind, base + h, span * GDN_CHUNK:(span + 1) * GDN_CHUNK] = 1.0
            e128[kind, base + h, h * HEAD_DIM:(h + 1) * HEAD_DIM] = 1.0
    return jnp.asarray(e64, dtype=BF16), jnp.asarray(e128, dtype=BF16)


def _gdn_prep(pf, conv_w, alog_pad, dt_pad):
    t = pf.shape[0]
    tm = 512
    cch = 2 * GDN_QK_HEADS * HEAD_DIM + GDN_V_HEADS * HEAD_DIM
    nqk = GDN_QK_HEADS * HEAD_DIM
    nv = GDN_V_HEADS * HEAD_DIM
    n64 = GDN_V_HEADS * GDN_CHUNK
    u = PF_QG * LANES // cch
    e64, e128 = _head_spread_matrices()
    lane = pl.BlockSpec((1, LANES), lambda i: (0, 0))
    row = lambda w: pl.BlockSpec((tm, w), lambda i: (i, 0))
    f32 = lambda w: jax.ShapeDtypeStruct((t, w), F32)
    return pl.pallas_call(
        functools.partial(_gprep_body, tm=tm),
        grid=(t // tm,),
        in_specs=[pl.BlockSpec((tm, cch), lambda i: (i, u)),
                  pl.BlockSpec((8, cch), lambda i: (jnp.maximum(i * (tm // 8) - 1, 0), u)),
                  pl.BlockSpec((GDN_CONV, cch), lambda i: (0, 0)),
                  pl.BlockSpec((tm, LANES), lambda i: (i, PF_SMALL)),
                  lane, lane,
                  pl.BlockSpec((2, LANES, n64), lambda i: (0, 0, 0)),
                  pl.BlockSpec((2, LANES, nv), lambda i: (0, 0, 0))],
        out_specs=[row(nqk), row(nqk), row(nv), row(n64), row(n64), row(nv), row(nv)],
        out_shape=[f32(nqk), f32(nqk), f32(nv), f32(n64), f32(n64), f32(nv), f32(nv)],
        compiler_params=_cparams(("arbitrary",)),
        name="gdn_prep",
    )(pf, pf, conv_w, pf, alog_pad, dt_pad, e64, e128)


GDN_PACK = 4


def _packed_unit_lower_inverse(a, eye_p, blk16_p, bd_mask):
    def block_diag(y):
        return jnp.where(bd_mask, jnp.concatenate([y] * GDN_PACK, axis=0), 0.0).astype(BF16)

    def pmm(x, y):
        xh, xl = _split2(x)
        yh = y.astype(BF16)
        yl = y - yh.astype(F32)
        ybh = block_diag(yh.astype(F32))
        return _dot(xh, ybh) + (_dot(xh, block_diag(yl)) + _dot(xl, ybh))

    ad = jnp.where(blk16_p, a, 0.0)
    ao = a - ad
    x2 = pmm(ad, ad)
    x4 = pmm(x2, x2)
    x8 = pmm(x4, x4)
    p = eye_p - ad
    p = p + pmm(p, x2)
    p = p + pmm(p, x4)
    p = p + pmm(p, x8)
    m = pmm(p, ao)
    m2 = pmm(m, m)
    t1 = p + pmm(m2, p)
    return t1 - pmm(m, t1)


def _gscan_body(q_ref, k_ref, v_ref, g64_ref, b64_ref, g128_ref, b128_ref, z_ref, ng_ref, o_ref, s_ref, *, chunks):
    c = GDN_CHUNK
    dh = HEAD_DIM
    n_grp = GDN_V_HEADS // GDN_PACK
    half = GDN_PACK * c

    @pl.when(pl.program_id(0) == 0)
    def _():
        s_ref[...] = jnp.zeros(s_ref.shape, F32)

    r_i = _iota((c, n_grp * half), 0)
    c_i = _iota((c, n_grp * half), 1) & (c - 1)
    tri = r_i >= c_i
    strict = r_i > c_i
    upper_f = (r_i <= c_i).astype(F32)
    r_p = _iota((c, half), 0)
    c_p = _iota((c, half), 1) & (c - 1)
    eye_p = (r_p == c_p).astype(F32)
    blk16_p = (r_p >> 4) == (c_p >> 4)
    bd_mask = (_iota((half, half), 0) >> 6) == (_iota((half, half), 1) >> 6)
    k_mask = (_iota((half, GDN_PACK * dh), 0) >> 6) == (_iota((half, GDN_PACK * dh), 1) >> 7)
    lower = (_iota((c, c), 0) >= _iota((c, c), 1)).astype(BF16)
    ones = jnp.ones((c, c), BF16)
    ng = ng_ref[...]
    zero_blk = jnp.zeros((c, dh), F32)

    def chunk(n, carry):
        rows = pl.ds(pl.multiple_of(n * c, c), c)
        k4 = k_ref[rows, :]
        q4 = q_ref[rows, :] * (dh ** -0.5)
        kbd = jnp.where(k_mask, jnp.concatenate([k4] * GDN_PACK, axis=0), 0.0).astype(BF16)
        kk = _dot_nt(k4.astype(BF16), kbd)
        qk_raw = _dot_nt(q4.astype(BF16), kbd)
        g64 = g64_ref[rows, :]
        gc64 = _dot_ones_l(lower, g64)
        gc_row = _dot_ones_l(ones, g64 * upper_f)
        decay = jnp.where(tri, jnp.exp(jnp.where(tri, gc64 - gc_row, 0.0)), 0.0)
        a_all = jnp.where(strict, jnp.concatenate([kk] * n_grp, axis=1) * decay * b64_ref[rows, :], 0.0)
        qk_all = jnp.where(tri, jnp.concatenate([qk_raw] * n_grp, axis=1) * decay, 0.0)
        gc128 = _dot_ones_l(lower, g128_ref[rows, :])
        egc = jnp.exp(gc128)
        g_last = gc128[c - 1:c]
        k_fade = jnp.exp(g_last - gc128)
        g_tot = jnp.exp(g_last)
        b128 = b128_ref[rows, :]
        v_all = v_ref[rows, :]
        z_all = z_ref[rows, :]
        for grp in range(n_grp):
            def pick(x, grp=grp):
                return jnp.concatenate([x[:, (n_grp * i + grp) * dh:(n_grp * i + grp + 1) * dh]
                                        for i in range(GDN_PACK)], axis=1)

            tinv = _packed_unit_lower_inverse(a_all[:, grp * half:(grp + 1) * half], eye_p, blk16_p, bd_mask)
            beta = pick(b128)
            eg = pick(egc)
            vb = pick(v_all) * beta
            kbe = k4 * beta * eg
            qd = q4 * eg
            kd = k4 * pick(k_fade)
            st = s_ref[grp]
            zero_pair = jnp.zeros((c, 2 * dh), F32)
            rhs_bd = jnp.concatenate(
                [jnp.concatenate([jnp.concatenate([vb[:, i * dh:(i + 1) * dh], kbe[:, i * dh:(i + 1) * dh]], axis=1)
                                  if j == i else zero_pair for j in range(GDN_PACK)], axis=1)
                 for i in range(GDN_PACK)], axis=0)
            th, tl = _split2(tinv)
            rh, rl = _split2(rhs_bd)
            uw_all = _dot(th, rh) + (_dot(th, rl) + _dot(tl, rh))
            r_out, v_new = [], []
            for i in range(GDN_PACK):
                hs = slice(i * dh, (i + 1) * dh)
                u = uw_all[:, 2 * i * dh:(2 * i + 1) * dh]
                w = uw_all[:, (2 * i + 1) * dh:(2 * i + 2) * dh]
                r = _dot(jnp.concatenate([w, qd[:, hs]], axis=0).astype(BF16), st[:, hs].astype(BF16))
                v_new.append(u - r[:c])
                r_out.append(r[c:])
            vn_bd = jnp.concatenate(
                [jnp.concatenate([v_new[i] if j == i else zero_blk for j in range(GDN_PACK)], axis=1)
                 for i in range(GDN_PACK)], axis=0).astype(BF16)
            o_all = jnp.concatenate(r_out, axis=1) + _dot(qk_all[:, grp * half:(grp + 1) * half].astype(BF16), vn_bd)
            k_stack = jnp.concatenate([kd[:, i * dh:(i + 1) * dh] for i in range(GDN_PACK)], axis=0)
            s_ref[grp] = st * pick(g_tot) + _dot(k_stack.T.astype(BF16), vn_bd)
            zg = pick(z_all)
            for i in range(GDN_PACK):
                hs = slice(i * dh, (i + 1) * dh)
                h = n_grp * i + grp
                o = o_all[:, hs]
                zh = zg[:, hs]
                o = o * lax.rsqrt(jnp.mean(o * o, -1, keepdims=True) + 1e-6) * ng * (zh * jax.nn.sigmoid(zh))
                o_ref[rows, h * dh:(h + 1) * dh] = o.astype(o_ref.dtype)
        return carry

    lax.fori_loop(0, chunks, chunk, 0)


def _gdn_scan(qn, kn, v, g64, b64, g128, b128, pf, norm_g):
    t = qn.shape[0]
    chunks = 4
    tm = chunks * GDN_CHUNK
    nqk = GDN_QK_HEADS * HEAD_DIM
    nv = GDN_V_HEADS * HEAD_DIM
    n64 = GDN_V_HEADS * GDN_CHUNK
    row = lambda w: pl.BlockSpec((tm, w), lambda i: (i, 0))
    return pl.pallas_call(
        functools.partial(_gscan_body, chunks=chunks),
        grid=(t // tm,),
        in_specs=[row(nqk), row(nqk), row(nv), row(n64), row(n64), row(nv), row(nv),
                  pl.BlockSpec((tm, nv), lambda i: (i, PF_Z * LANES // nv)),
                  pl.BlockSpec((1, HEAD_DIM), lambda i: (0, 0))],
        out_specs=row(nv),
        out_shape=jax.ShapeDtypeStruct((t, nv), BF16),
        scratch_shapes=[pltpu.VMEM((GDN_V_HEADS // GDN_PACK, HEAD_DIM, GDN_PACK * HEAD_DIM), F32)],
        compiler_params=_cparams(("arbitrary",)),
        name="gdn_scan",
    )(qn, kn, v, g64, b64, g128, b128, pf, norm_g)


def _post_body(on_ref, og_ref, w_ref, x_ref, gm_ref, lg_ref, lb_ref, sh_ref, sc_ref, rw_ref, rb_ref, tril_ref,
               x1_ref, hp_ref, ti_ref, gt_ref, rk_ref, cnt_ref, carry_ref):
    half = D_MODEL // 2

    @pl.when(pl.program_id(0) == 0)
    def _():
        carry_ref[...] = jnp.zeros(carry_ref.shape, F32)

    mix = _dot(on_ref[...], w_ref[0:half, :]) + _dot(og_ref[...], w_ref[half:, :])
    x1 = _ln_plain(DEEPNORM_ALPHA * x_ref[...] + gm_ref[...] * mix, 1e-5) * lg_ref[...] + lb_ref[...]
    x1_ref[...] = x1
    h2 = _ln_plain(x1, 1e-6) * (1.0 + sc_ref[...]) + sh_ref[...]
    lo = lax.bitcast_convert_type(h2[:, :half].astype(BF16).astype(F32), U32) >> 16
    hi = lax.bitcast_convert_type(h2[:, half:].astype(BF16).astype(F32), U32) & jnp.uint32(0xFFFF0000)
    hp_ref[...] = lo | hi
    sc = _dot3(h2, rw_ref[...]) + rb_ref[...]
    lane = _iota((1, LANES), 1).astype(F32)
    hot = jnp.zeros(sc.shape, F32)
    vals, idxs = [], []
    for _ in range(TOP_K):
        m = jnp.max(sc, -1, keepdims=True)
        idx = jnp.min(jnp.where(sc == m, lane, float(LANES)), -1, keepdims=True)
        pick = lane == idx
        hot = jnp.where(pick, 1.0, hot)
        sc = jnp.where(pick, -jnp.inf, sc)
        vals.append(m)
        idxs.append(idx)
    ex = [jnp.exp(v - vals[0]) for v in vals]
    den = (ex[0] + ex[1]) + (ex[2] + ex[3])
    before = carry_ref[...] + _dot(tril_ref[...], hot.astype(BF16))
    ti = jnp.zeros(sc.shape, F32)
    gt = jnp.zeros(sc.shape, F32)
    rk = jnp.zeros(sc.shape, F32)
    for k in range(TOP_K):
        at_k = lane == float(k)
        ti = jnp.where(at_k, idxs[k], ti)
        gt = jnp.where(at_k, ex[k] / den, gt)
        rk = jnp.where(at_k, jnp.sum(jnp.where(lane == idxs[k], before, 0.0), -1, keepdims=True), rk)
    ti_ref[...] = ti.astype(I32)
    gt_ref[...] = gt
    rk_ref[...] = rk.astype(I32)
    carry_ref[...] = carry_ref[...] + jnp.sum(hot, axis=0, keepdims=True)
    cnt_ref[...] = carry_ref[...]


def _post_mixer(o_nsa, o_gdn, w_out, x, gm1, ln_g, ln_b, sh_f, sc_f, rw_pad, rb_pad, tril):
    t, d = x.shape
    tm = tril.shape[0]
    half = d // 2
    vec = pl.BlockSpec((1, d), lambda i: (0, 0))
    lane_vec = pl.BlockSpec((1, LANES), lambda i: (0, 0))
    row = lambda w: pl.BlockSpec((tm, w), lambda i: (i, 0))
    return pl.pallas_call(
        _post_body,
        grid=(t // tm,),
        in_specs=[row(half), row(half), pl.BlockSpec((d, d), lambda i: (0, 0)), row(d),
                  vec, vec, vec, vec, vec,
                  pl.BlockSpec((d, LANES), lambda i: (0, 0)), lane_vec,
                  pl.BlockSpec((tm, tm), lambda i: (0, 0))],
        out_specs=[row(d), row(half), row(LANES), row(LANES), row(LANES), lane_vec],
        out_shape=[jax.ShapeDtypeStruct((t, d), F32), jax.ShapeDtypeStruct((t, half), U32),
                   jax.ShapeDtypeStruct((t, LANES), I32), jax.ShapeDtypeStruct((t, LANES), F32),
                   jax.ShapeDtypeStruct((t, LANES), I32), jax.ShapeDtypeStruct((1, LANES), F32)],
        scratch_shapes=[pltpu.VMEM((1, LANES), F32)],
        compiler_params=_cparams(("arbitrary",)),
        name="post_mixer_router",
    )(o_nsa, o_gdn, w_out, x, gm1, ln_g, ln_b, sh_f, sc_f, rw_pad, rb_pad, tril)


def _row_copy(src_ref, dst_ref, sem):
    return pltpu.make_async_copy(src_ref, dst_ref, sem)


def _dispatch_body(dest_ref, hp_ref, xs_in_ref, xs_ref, sem, *, tm):
    del xs_in_ref

    def issue(r, carry):
        for k in range(TOP_K):
            _row_copy(hp_ref.at[pl.ds(r, 1)], xs_ref.at[pl.ds(dest_ref[r * TOP_K + k], 1)], sem).start()
        return carry

    def drain(r, carry):
        for k in range(TOP_K):
            _row_copy(hp_ref.at[pl.ds(0, 1)], xs_ref.at[pl.ds(0, 1)], sem).wait()
        return carry

    lax.fori_loop(0, tm, issue, 0)
    lax.fori_loop(0, tm, drain, 0)


def _dispatch(dest, hp, rows):
    t, w = hp.shape
    tm = 256
    zeros = jnp.zeros((rows, w), hp.dtype)
    return pl.pallas_call(
        functools.partial(_dispatch_body, tm=tm),
        grid=(t // tm,),
        in_specs=[pl.BlockSpec((tm * TOP_K,), lambda i: (i,), memory_space=pltpu.SMEM),
                  pl.BlockSpec((tm, w), lambda i: (i, 0)),
                  pl.BlockSpec(memory_space=pl.ANY)],
        out_specs=pl.BlockSpec(memory_space=pl.ANY),
        out_shape=jax.ShapeDtypeStruct((rows, w), hp.dtype),
        scratch_shapes=[pltpu.SemaphoreType.DMA(())],
        input_output_aliases={2: 0},
        compiler_params=_cparams(("arbitrary",)),
        name="moe_dispatch",
    )(dest, hp, zeros)


def _deinterleave_body(w_ref, p_ref, g_ref, l_ref):
    perm = p_ref[...]
    for c in range(w_ref.shape[2] // (2 * LANES)):
        r = _dot(w_ref[0, :, c * 2 * LANES:(c + 1) * 2 * LANES].astype(BF16), perm)
        g_ref[0, :, c * LANES:(c + 1) * LANES] = r[:, :LANES].astype(g_ref.dtype)
        l_ref[0, :, c * LANES:(c + 1) * LANES] = r[:, LANES:].astype(l_ref.dtype)


def _split_gate_up(w_gu_all, layer):
    nl, e, d, n = w_gu_all.shape
    w_gu = w_gu_all.reshape(nl * e, d, n)
    first = layer * e
    tr = 512
    perm = np.zeros((2 * LANES, 2 * LANES), np.float32)
    perm[np.arange(0, 2 * LANES, 2), np.arange(LANES)] = 1.0
    perm[np.arange(1, 2 * LANES, 2), LANES + np.arange(LANES)] = 1.0
    out = jax.ShapeDtypeStruct((e, d, n // 2), BF16)
    ospec = pl.BlockSpec((1, tr, n // 2), lambda i, j: (i, j, 0))
    return pl.pallas_call(
        _deinterleave_body,
        grid=(e, d // tr),
        in_specs=[pl.BlockSpec((1, tr, n), lambda i, j: (first + i, j, 0)),
                  pl.BlockSpec((2 * LANES, 2 * LANES), lambda i, j: (0, 0))],
        out_specs=[ospec, ospec],
        out_shape=[out, out],
        compiler_params=_cparams(("arbitrary", "arbitrary")),
        name="moe_split_gate_up",
    )(w_gu, jnp.asarray(perm, dtype=BF16))


def _gu_body(be_ref, xs_ref, wg_ref, wl_ref, bg_ref, bl_ref, o_ref):
    del be_ref
    half = D_MODEL // 2
    u = xs_ref[...]
    xa = lax.bitcast_convert_type(u << 16, F32).astype(BF16)
    xb = lax.bitcast_convert_type(u & jnp.uint32(0xFFFF0000), F32).astype(BF16)
    tn = 512
    for c in range(D_FF // tn):
        cs = slice(c * tn, (c + 1) * tn)
        hg = _dot(xa, wg_ref[0, 0:half, cs]) + _dot(xb, wg_ref[0, half:, cs]) + bg_ref[0, :, cs]
        hl = _dot(xa, wl_ref[0, 0:half, cs]) + _dot(xb, wl_ref[0, half:, cs]) + bl_ref[0, :, cs]
        x_glu = jnp.minimum(hg, SWIGLU_LIMIT)
        x_lin = jnp.clip(hl, -SWIGLU_LIMIT, SWIGLU_LIMIT)
        o_ref[:, cs] = (x_glu * jax.nn.sigmoid(SWIGLU_ALPHA * x_glu) * (x_lin + 1.0)).astype(o_ref.dtype)


def _expert_gu(blk_exp, xs, wg, wl, bg, bl):
    rows, half = xs.shape
    nb = rows // EXPERT_ROWS
    wspec = pl.BlockSpec((1, D_MODEL, D_FF), lambda b, be: (be[b], 0, 0))
    bspec = pl.BlockSpec((1, 1, D_FF), lambda b, be: (be[b], 0, 0))
    grid_spec = pltpu.PrefetchScalarGridSpec(
        num_scalar_prefetch=1,
        grid=(nb,),
        in_specs=[pl.BlockSpec((EXPERT_ROWS, half), lambda b, be: (b, 0)), wspec, wspec, bspec, bspec],
        out_specs=pl.BlockSpec((EXPERT_ROWS, D_FF), lambda b, be: (b, 0)),
    )
    return pl.pallas_call(
        _gu_body,
        grid_spec=grid_spec,
        out_shape=jax.ShapeDtypeStruct((rows, D_FF), BF16),
        compiler_params=_cparams(("arbitrary",)),
        name="moe_gate_up",
    )(blk_exp, xs, wg, wl, bg, bl)


def _dn_body(be_ref, h_ref, w_ref, b_ref, o_ref):
    del be_ref
    o_ref[...] = _dot(h_ref[...], w_ref[0]) + b_ref[0]


def _expert_dn(blk_exp, hact, wd, bd):
    rows = hact.shape[0]
    nb = rows // EXPERT_ROWS
    grid_spec = pltpu.PrefetchScalarGridSpec(
        num_scalar_prefetch=1,
        grid=(nb,),
        in_specs=[pl.BlockSpec((EXPERT_ROWS, D_FF), lambda b, be: (b, 0)),
                  pl.BlockSpec((1, D_FF, D_MODEL), lambda b, be: (be[b], 0, 0)),
                  pl.BlockSpec((1, 1, D_MODEL), lambda b, be: (be[b], 0, 0))],
        out_specs=pl.BlockSpec((EXPERT_ROWS, D_MODEL), lambda b, be: (b, 0)),
    )
    return pl.pallas_call(
        _dn_body,
        grid_spec=grid_spec,
        out_shape=jax.ShapeDtypeStruct((rows, D_MODEL), F32),
        compiler_params=_cparams(("arbitrary",)),
        name="moe_down",
    )(blk_exp, hact, wd, bd)


def _combine_body(dest_ref, y_ref, gt_ref, x1_ref, gf_ref, lg_ref, lb_ref, o_ref, buf_ref, sem, *, tm):
    def issue(r, carry):
        for k in range(TOP_K):
            _row_copy(y_ref.at[pl.ds(dest_ref[r * TOP_K + k], 1)], buf_ref.at[k, pl.ds(r, 1)], sem).start()
        return carry

    def drain(r, carry):
        for k in range(TOP_K):
            _row_copy(y_ref.at[pl.ds(0, 1)], buf_ref.at[k, pl.ds(0, 1)], sem).wait()
        return carry

    lax.fori_loop(0, tm, issue, 0)
    lax.fori_loop(0, tm, drain, 0)
    gt = gt_ref[...]
    y = (gt[:, 0:1] * buf_ref[0] + gt[:, 1:2] * buf_ref[1]) + (gt[:, 2:3] * buf_ref[2] + gt[:, 3:4] * buf_ref[3])
    o_ref[...] = _ln_plain(DEEPNORM_ALPHA * x1_ref[...] + gf_ref[...] * y, 1e-5) * lg_ref[...] + lb_ref[...]


def _combine(dest, y, gt, x1, gf1, ln_g, ln_b):
    t, d = x1.shape
    tm = 256
    vec = pl.BlockSpec((1, d), lambda i: (0, 0))
    return pl.pallas_call(
        functools.partial(_combine_body, tm=tm),
        grid=(t // tm,),
        in_specs=[pl.BlockSpec((tm * TOP_K,), lambda i: (i,), memory_space=pltpu.SMEM),
                  pl.BlockSpec(memory_space=pl.ANY),
                  pl.BlockSpec((tm, LANES), lambda i: (i, 0)),
                  pl.BlockSpec((tm, d), lambda i: (i, 0)), vec, vec, vec],
        out_specs=pl.BlockSpec((tm, d), lambda i: (i, 0)),
        out_shape=jax.ShapeDtypeStruct((t, d), F32),
        scratch_shapes=[pltpu.VMEM((TOP_K, tm, d), F32), pltpu.SemaphoreType.DMA(())],
        compiler_params=_cparams(("arbitrary",)),
        name="moe_combine",
    )(dest, y, gt, x1, gf1, ln_g, ln_b)


def _overlap_matrix(t):
    nc_pad = t // CMP_STRIDE
    n_c = (t - CMP_BLOCK) // CMP_STRIDE + 1
    n_s = t // SEL_BLOCK
    nsp = -(-n_s // LANES) * LANES
    ci = np.arange(nc_pad)[:, None] * CMP_STRIDE
    sj = np.arange(nsp)[None, :] * SEL_BLOCK
    ov = (ci < sj + SEL_BLOCK) & (ci + CMP_BLOCK > sj)
    ov &= (np.arange(nc_pad)[:, None] < n_c) & (np.arange(nsp)[None, :] < n_s)
    return jnp.asarray(ov.astype(np.float32), dtype=BF16)


def _block_onehot():
    e = (np.arange(SEL_SUPER)[:, None] // SEL_BLOCK) == np.arange(LANES)[None, :]
    return jnp.asarray(e.astype(np.float32), dtype=BF16)


def _in_proj_weights(w_in):
    nq = NSA_HEADS * HEAD_DIM
    nkv = NSA_GROUPS * HEAD_DIM
    o = np.cumsum([0, nq, nkv, nkv, nkv, nkv, nkv, nkv, 3 * NSA_HEADS,
                   GDN_QK_HEADS * HEAD_DIM, GDN_QK_HEADS * HEAD_DIM, GDN_V_HEADS * HEAD_DIM,
                   GDN_V_HEADS * HEAD_DIM, GDN_V_HEADS, GDN_V_HEADS]).tolist()
    q, kc, vc, ks, vs, kw, vw, gate, qg, kg, vg, z, b_l, a_l = [w_in[:, o[i]:o[i + 1]] for i in range(14)]
    wb = jnp.concatenate([q, ks, vs, kw, vw], axis=1).astype(BF16)
    small = jnp.concatenate([gate, b_l, a_l], axis=1)
    pad = jnp.zeros((w_in.shape[0], 4 * LANES - small.shape[1]), w_in.dtype)
    wf = jnp.concatenate([qg, kg, vg, z, kc, vc, small, pad], axis=1).astype(BF16)
    cs_b = jnp.concatenate([jnp.full((1, nq), HEAD_DIM ** -0.5 * LOG2E, F32), jnp.ones((1, wb.shape[1] - nq), F32)],
                           axis=1)
    cs_f = jnp.ones((1, wf.shape[1]), F32)
    return wb, cs_b, wf, cs_f


def _lane_pad(v, start):
    return jnp.zeros((1, LANES), F32).at[0, start:start + v.shape[0]].set(v)


def _layer(x, mod, w_in, cmp_pos, cmp_w1, cmp_w2, nsa_out_g, conv_w, a_log, dt_bias, gdn_norm_g, w_out,
           ln_mix_g, ln_mix_b, router_w, router_b, w_gu_all, layer, b_gu, w_dn, b_dn, ln_ffn_g, ln_ffn_b, consts):
    t, d = x.shape
    ov, etab, tril = consts
    sh_m, sc_m, g_m, sh_f, sc_f, g_f = [mod[:, i * d:(i + 1) * d] for i in range(6)]

    h = _lnmod(x, sh_m, sc_m)
    wb, cs_b, wf, cs_f = _in_proj_weights(w_in)
    pb = _matmul(h, wb, cs_b, BF16)
    pf = _matmul(h, wf, cs_f, F32)
    cc = _compress(pf, cmp_pos, cmp_w1, cmp_w2)
    n_sel = min(SEL_COUNT, t // SEL_BLOCK)
    o_cmp, mb = _cmp_select(pb, cc, ov, n_sel)
    o_sel = _sel_attention(pb, mb, etab)
    o_nsa = _win_combine(pb, pf, o_cmp, o_sel, nsa_out_g.reshape(1, -1))
    qn, kn, v, g64, b64, g128, b128 = _gdn_prep(pf, conv_w.reshape(GDN_CONV, -1), _lane_pad(a_log, SM_A),
                                                _lane_pad(dt_bias, SM_A))
    o_gdn = _gdn_scan(qn, kn, v, g64, b64, g128, b128, pf, gdn_norm_g.reshape(1, -1))

    rw_pad = jnp.zeros((d, LANES), F32).at[:, :N_EXPERTS].set(router_w)
    rb_pad = jnp.full((1, LANES), -jnp.inf, F32).at[0, :N_EXPERTS].set(router_b)
    x1, hp, ti, gt, rk, cnt = _post_mixer(o_nsa, o_gdn, w_out.astype(BF16), x, 1.0 + g_m, ln_mix_g.reshape(1, d),
                                           ln_mix_b.reshape(1, d), sh_f, sc_f, rw_pad, rb_pad, tril)

    counts = cnt[0, :N_EXPERTS].astype(I32)
    padded = (counts + EXPERT_ROWS - 1) // EXPERT_ROWS * EXPERT_ROWS
    pad_end = jnp.cumsum(padded)
    pad_start = pad_end - padded
    dest = (pad_start[ti[:, :TOP_K]] + rk[:, :TOP_K]).reshape(-1)
    nb = t * TOP_K // EXPERT_ROWS + N_EXPERTS
    blk_start = jnp.arange(nb, dtype=I32) * EXPERT_ROWS
    blk_exp = jnp.minimum(jnp.sum((pad_end[None, :] <= blk_start[:, None]).astype(I32), axis=1), N_EXPERTS - 1)

    xs = _dispatch(dest, hp, nb * EXPERT_ROWS)
    wg, wl = _split_gate_up(w_gu_all, layer)
    bg = b_gu[:, None, 0::2]
    bl = b_gu[:, None, 1::2]
    hact = _expert_gu(blk_exp, xs, wg, wl, bg, bl)
    y = _expert_dn(blk_exp, hact, w_dn.astype(BF16), b_dn[:, None, :])
    return _combine(dest, y, gt, x1, 1.0 + g_f, ln_ffn_g.reshape(1, d), ln_ffn_b.reshape(1, d))


def kernel(x, c, w_in, cmp_pos_k, cmp_pos_v, cmp_k_w1, cmp_k_w2, cmp_v_w1, cmp_v_w2, nsa_out_g, gdn_conv_w,
           gdn_a_log, gdn_dt_bias, gdn_norm_g, w_out, w_ada, b_ada, ln_mix_g, ln_mix_b, router_w, router_b,
           w_gu, b_gu, w_dn, b_dn, ln_ffn_g, ln_ffn_b):
    bsz, t, d = x.shape
    assert bsz == 1 and d == D_MODEL and t % 1024 == 0
    nl = w_in.shape[0]
    consts = (_overlap_matrix(t), _block_onehot(),
              jnp.asarray(np.tril(np.ones((512, 512), np.float32), -1), dtype=BF16))
    mod = _ada(c, w_ada, b_ada)
    xt = x.reshape(t, d)
    for l in range(nl):
        pos = jnp.stack([cmp_pos_k[l], cmp_pos_v[l]])
        w1 = jnp.stack([cmp_k_w1[l], cmp_v_w1[l]]).reshape(2, CMP_BLOCK, HEAD_DIM, CMP_HIDDEN).astype(BF16)
        w2 = jnp.stack([cmp_k_w2[l], cmp_v_w2[l]]).astype(BF16)
        xt = _layer(xt, mod[l], w_in[l], pos, w1, w2, nsa_out_g[l], gdn_conv_w[l], gdn_a_log[l], gdn_dt_bias[l],
                    gdn_norm_g[l], w_out[l], ln_mix_g[l], ln_mix_b[l], router_w[l], router_b[l], w_gu, l, b_gu[l],
                    w_dn[l], b_dn[l], ln_ffn_g[l], ln_ffn_b[l], consts)
    return xt.reshape(bsz, t, d)
```
